```python
import math
import jax, jax.numpy as jnp
from jax import lax
import numpy as np

D_MODEL = 2048
BATCH = 1
SEQ = 8192
DEPTH = 4

CHUNK = 64
N_MIXERS = 2
N_A = (DEPTH + 1) // 2
N_B = DEPTH // 2
RMS_EPS = 1e-6

A_HEAD = 64
A_HEADS = D_MODEL // A_HEAD
A_DECAY_LORA = max(32, int(round(1.8 * D_MODEL ** 0.5 / 32)) * 32)
A_AAA_LORA = max(32, int(round(1.8 * D_MODEL ** 0.5 / 32)) * 32)
A_MV_LORA = max(32, int(round(1.3 * D_MODEL ** 0.5 / 32)) * 32)
A_GATE_LORA = max(32, int(round(0.6 * D_MODEL ** 0.8 / 32)) * 32)
A_GN_EPS = A_HEAD * 1e-5

B_HEADS = 16
B_HEAD_DIM = 128
B_KV_RANK = 512
B_IDX_HEADS = 8
B_IDX_DIM = 128
TOPK_MAX = 256
Q_BLOCK = 128
B_Q_W = B_HEADS * B_HEAD_DIM
B_QI_W = B_IDX_HEADS * B_IDX_DIM
B_IN_W = B_Q_W + B_KV_RANK + B_QI_W + B_IDX_DIM + B_IDX_HEADS
B_SCALE = B_HEAD_DIM ** -0.5
B_IDX_W_SCALE = (B_IDX_HEADS ** -0.5) * (B_IDX_DIM ** -0.5)

D_FF = ((8 * D_MODEL // 3 + 255) // 256) * 256

kernel_name = 'hybrid_rwkv7_dsa_adaln_encoder'


def _rmsnorm(x):
    xf = x.astype(jnp.float32)
    return (xf * lax.rsqrt(jnp.mean(xf * xf, axis=-1, keepdims=True) + RMS_EPS)).astype(x.dtype)


def _modulate(x, shift, scale):
    return _rmsnorm(x) * (1 + scale[:, None, :]) + shift[:, None, :]


def _rwkv7_scan(r, w, k, v, kk, a):
    B, T, H, N = r.shape

    def step(S, inp):
        r_t, w_t, k_t, v_t, kk_t, a_t = inp
        sa = jnp.einsum('bhij,bhj->bhi', S, -kk_t)
        S = (S * w_t[:, :, None, :]
             + sa[..., None] * (kk_t * a_t)[:, :, None, :]
             + v_t[..., None] * k_t[:, :, None, :])
        y = jnp.einsum('bhij,bhj->bhi', S, r_t)
        return S, y

    xs = tuple(jnp.moveaxis(t, 1, 0) for t in (r, w, k, v, kk, a))
    S0 = jnp.zeros((B, H, N, N), jnp.float32)
    _, ys = lax.scan(step, S0, xs)
    return jnp.moveaxis(ys, 0, 1)


def _rwkv7_mixer(h, v_first, mu, w_rkv, w0, w1, w2, a0, a1, a2, g1, g2,
                 k_k, k_a, r_k, ln_w, ln_b, w_o, vres):
    B, T, D = h.shape
    f32 = jnp.float32
    h_prev = jnp.pad(h, ((0, 0), (1, 0), (0, 0)))[:, :T]
    dx = h_prev - h
    xr, xw, xk, xv, xa, xg = (h + dx * mu[i] for i in range(6))
    r = xr @ w_rkv[0]
    k = xk @ w_rkv[1]
    v = xv @ w_rkv[2]
    w_log = -jax.nn.softplus(-(w0 + jnp.tanh(xw @ w1) @ w2).astype(f32)) - 0.5
    decay = jnp.exp(-jnp.exp(w_log))
    if vres is None:
        v_first = v
    else:
        v0, v1, v2 = vres
        v = v + (v_first - v) * jax.nn.sigmoid(v0 + (xv @ v1) @ v2)
    a = jax.nn.sigmoid((a0 + (xa @ a1) @ a2).astype(f32))
    g = jax.nn.sigmoid(xg @ g1) @ g2

    def heads(t):
        return t.astype(f32).reshape(B, T, A_HEADS, A_HEAD)

    kk = heads(k * k_k)
    kk = kk / jnp.maximum(jnp.linalg.norm(kk, axis=-1, keepdims=True), 1e-12)
    k = k.astype(f32) * (1 + (a - 1) * k_a)
    r_h, k_h, v_h, w_h, a_h = heads(r), heads(k), heads(v), heads(decay), heads(a)
    y = _rwkv7_scan(r_h, w_h, k_h, v_h, kk, a_h)
    mean = jnp.mean(y, axis=-1, keepdims=True)
    var = jnp.mean(jnp.square(y - mean), axis=-1, keepdims=True)
    y = ((y - mean) * lax.rsqrt(var + A_GN_EPS)).reshape(B, T, D) * ln_w + ln_b
    bonus = jnp.sum(r_h * k_h * r_k, axis=-1, keepdims=True) * v_h
    y = (y + bonus.reshape(B, T, D)) * g
    return y.astype(h.dtype) @ w_o, v_first


def _dsa_mixer(h, w_in, kv_norm, kidx_norm, w_uk, w_uv, w_o):
    B, T, D = h.shape
    f32 = jnp.float32
    proj = h @ w_in
    q, c_kv, q_idx, k_idx, w_idx = jnp.split(
        proj, [B_Q_W, B_Q_W + B_KV_RANK, B_Q_W + B_KV_RANK + B_QI_W,
               B_Q_W + B_KV_RANK + B_QI_W + B_IDX_DIM], axis=-1)
    q = q.reshape(B, T, B_HEADS, B_HEAD_DIM)
    c_kv = _rmsnorm(c_kv) * kv_norm
    q_idx = q_idx.reshape(B, T, B_IDX_HEADS, B_IDX_DIM)
    k_idx = _rmsnorm(k_idx) * kidx_norm
    w_idx = w_idx * B_IDX_W_SCALE
    topk = min(TOPK_MAX, T // 4)
    nb = T // Q_BLOCK
    key_pos = jnp.arange(T, dtype=jnp.int32)
    q_pos = key_pos.reshape(nb, Q_BLOCK)

    def to_blocks(t):
        return jnp.moveaxis(t.reshape((B, nb, Q_BLOCK) + t.shape[2:]), 1, 0)

    def block(args):
        q_b, qi_b, wi_b, pos_b = args
        limit = (pos_b // CHUNK + 1) * CHUNK
        rel = jnp.maximum(jnp.einsum('bqhd,bsd->bqhs', qi_b, k_idx), 0)
        score = jnp.einsum('bqh,bqhs->bqs', wi_b, rel).astype(f32)
        admissible = key_pos[None, :] < limit[:, None]
        score = jnp.where(admissible[None], score, -jnp.inf)
        _, sel = lax.top_k(score, topk)
        valid = sel < limit[None, :, None]
        ckv_sel = jax.vmap(lambda kv, ix: kv[ix])(c_kv, sel)
        q_lat = jnp.einsum('bqhd,hdc->bqhc', q_b, w_uk)
        logits = jnp.einsum('bqhc,bqkc->bqhk', q_lat, ckv_sel).astype(f32) * B_SCALE
        logits = jnp.where(valid[:, :, None, :], logits, -jnp.inf)
        p = jax.nn.softmax(logits, axis=-1).astype(h.dtype)
        o_lat = jnp.einsum('bqhk,bqkc->bqhc', p, ckv_sel)
        o = jnp.einsum('bqhc,hcd->bqhd', o_lat, w_uv)
        return o.reshape(B, Q_BLOCK, B_Q_W)

    o = lax.map(block, (to_blocks(q), to_blocks(q_idx), to_blocks(w_idx), q_pos))
    o = jnp.moveaxis(o, 0, 1).reshape(B, T, B_Q_W)
    return o @ w_o


def _swiglu(h, w1, w3, w2):
    return (jax.nn.silu(h @ w1) * (h @ w3)) @ w2


def setup_inputs(seed: int = 0) -> dict:
    key = jax.random.key(seed)
    ks = iter(jax.random.split(key, 40))

    def nrm(shape, scale):
        return scale * jax.random.normal(next(ks), shape, jnp.float32)

    def unif(shape, lo, hi):
        return jax.random.uniform(next(ks), shape, jnp.float32, lo, hi)

    D = D_MODEL
    n_res = N_A - 1
    return {
        'x': nrm((BATCH, SEQ, D), 1.0),
        'c': nrm((BATCH, D), 1.0),
        'ada_w': nrm((DEPTH, D, 6 * D), D ** -0.5),
        'ada_b': nrm((DEPTH, 6 * D), 0.02),
        'a_mu': unif((N_A, 6, D), 0.0, 1.0),
        'a_w_rkv': nrm((N_A, 3, D, D), D ** -0.5),
        'a_w0': unif((N_A, D), -6.0, -1.0),
        'a_w1': nrm((N_A, D, A_DECAY_LORA), D ** -0.5),
        'a_w2': nrm((N_A, A_DECAY_LORA, D), 0.1 * A_DECAY_LORA ** -0.5),
        'a_a0': nrm((N_A, D), 0.1),
        'a_a1': nrm((N_A, D, A_AAA_LORA), D ** -0.5),
        'a_a2': nrm((N_A, A_AAA_LORA, D), 0.5 * A_AAA_LORA ** -0.5),
        'a_v0': nrm((n_res, D), 0.1),
        'a_v1': nrm((n_res, D, A_MV_LORA), D ** -0.5),
        'a_v2': nrm((n_res, A_MV_LORA, D), 0.5 * A_MV_LORA ** -0.5),
        'a_g1': nrm((N_A, D, A_GATE_LORA), D ** -0.5),
        'a_g2': nrm((N_A, A_GATE_LORA, D), A_GATE_LORA ** -0.5),
        'a_k_k': 0.85 + nrm((N_A, D), 0.05),
        'a_k_a': 1.0 + nrm((N_A, D), 0.05),
        'a_r_k': nrm((N_A, A_HEADS, A_HEAD), 0.1),
        'a_ln_w': 1.0 + nrm((N_A, D), 0.05),
        'a_ln_b': nrm((N_A, D), 0.02),
        'a_w_o': nrm((N_A, D, D), D ** -0.5),
        'b_w_in': nrm((N_B, D, B_IN_W), D ** -0.5),
        'b_kv_norm': 1.0 + nrm((N_B, B_KV_RANK), 0.05),
        'b_kidx_norm': 1.0 + nrm((N_B, B_IDX_DIM), 0.05),
        'b_w_uk': nrm((N_B, B_HEADS, B_HEAD_DIM, B_KV_RANK), B_KV_RANK ** -0.5),
        'b_w_uv': nrm((N_B, B_HEADS, B_KV_RANK, B_HEAD_DIM), B_KV_RANK ** -0.5),
        'b_w_o': nrm((N_B, B_Q_W, D), B_Q_W ** -0.5),
        'f_w1': nrm((DEPTH, D, D_FF), D ** -0.5),
        'f_w3': nrm((DEPTH, D, D_FF), D ** -0.5),
        'f_w2': nrm((DEPTH, D_FF, D), D_FF ** -0.5),
        'final_norm': 1.0 + nrm((D,), 0.05),
    }


def reference(x, c, ada_w, ada_b, a_mu, a_w_rkv, a_w0, a_w1, a_w2, a_a0, a_a1, a_a2,
              a_v0, a_v1, a_v2, a_g1, a_g2, a_k_k, a_k_a, a_r_k, a_ln_w, a_ln_b, a_w_o,
              b_w_in, b_kv_norm, b_kidx_norm, b_w_uk, b_w_uv, b_w_o,
              f_w1, f_w3, f_w2, final_norm):
    mod_all = jnp.einsum('bd,lde->lbe', jax.nn.silu(c), ada_w) + ada_b[:, None, :]
    v_first = None
    for i in range(DEPTH):
        shift1, scale1, gate1, shift2, scale2, gate2 = jnp.split(mod_all[i], 6, axis=-1)
        h = _modulate(x, shift1, scale1)
        j = i // N_MIXERS
        if i % N_MIXERS == 0:
            vres = None if j == 0 else (a_v0[j - 1], a_v1[j - 1], a_v2[j - 1])
            y, v_first = _rwkv7_mixer(h, v_first, a_mu[j], a_w_rkv[j], a_w0[j], a_w1[j], a_w2[j],
                                      a_a0[j], a_a1[j], a_a2[j], a_g1[j], a_g2[j],
                                      a_k_k[j], a_k_a[j], a_r_k[j], a_ln_w[j], a_ln_b[j],
                                      a_w_o[j], vres)
        else:
            y = _dsa_mixer(h, b_w_in[j], b_kv_norm[j], b_kidx_norm[j], b_w_uk[j], b_w_uv[j], b_w_o[j])
        x = x + gate1[:, None, :] * y
        h = _modulate(x, shift2, scale2)
        x = x + gate2[:, None, :] * _swiglu(h, f_w1[i], f_w3[i], f_w2[i])
    return _rmsnorm(x) * final_norm
```

```python
import functools

import jax
import jax.numpy as jnp
from jax import lax
from jax.experimental import pallas as pl
from jax.experimental.pallas import tpu as pltpu

F32 = jnp.float32
BF16 = jnp.bfloat16
I32 = jnp.int32

D_MODEL = 2048
DEPTH = 4
CHUNK = 64
RMS_EPS = 1e-6
A_HEAD = 64
A_HEADS = D_MODEL // A_HEAD
A_GN_EPS = A_HEAD * 1e-5
B_HEADS = 16
B_HEAD_DIM = 128
B_KV_RANK = 512
B_IDX_HEADS = 8
B_IDX_DIM = 128
TOPK_MAX = 256
Q_BLOCK = 128
B_Q_W = B_HEADS * B_HEAD_DIM
B_QI_W = B_IDX_HEADS * B_IDX_DIM
B_SCALE = B_HEAD_DIM ** -0.5
B_IDX_W_SCALE = (B_IDX_HEADS ** -0.5) * (B_IDX_DIM ** -0.5)
D_FF = 5632

LANES = 128
VMEM_LIMIT = 56 * 1024 * 1024
INT_MIN = -(2 ** 31)
INT_MAX = 2 ** 31 - 1
NEG_BIG = -1e30

_NT = (((1,), (1,)), ((), ()))
_TN = (((0,), (0,)), ((), ()))


def _cparams(sem):
    return pltpu.CompilerParams(dimension_semantics=sem, vmem_limit_bytes=VMEM_LIMIT)


def _dot(a, b):
    return jnp.dot(a, b, preferred_element_type=F32)


def _dot_nt(a, b):
    return lax.dot_general(a, b, _NT, preferred_element_type=F32)


def _dot_tn(a, b):
    return lax.dot_general(a, b, _TN, preferred_element_type=F32)


def _modulate(x, shift, scale):
    ms = jnp.mean(x * x, axis=-1, keepdims=True)
    return x * lax.rsqrt(ms + RMS_EPS) * (1.0 + scale) + shift


def _sigmoid(x):
    return 1.0 / (1.0 + jnp.exp(-x))


def _ada_kernel(c_ref, w_ref, b_ref, o_ref):
    c = c_ref[...]
    s = c * _sigmoid(c)
    o_ref[0] = jnp.sum(s * w_ref[0], axis=0, keepdims=True) + b_ref[0]


def _ada(c, ada_w, ada_b):
    depth, d, n = ada_w.shape
    tn = 1024
    return pl.pallas_call(
        _ada_kernel,
        grid=(depth, n // tn),
        in_specs=[
            pl.BlockSpec((d, 1), lambda l, j: (0, 0)),
            pl.BlockSpec((1, d, tn), lambda l, j: (l, 0, j)),
            pl.BlockSpec((1, 1, tn), lambda l, j: (l, 0, j)),
        ],
        out_specs=pl.BlockSpec((1, 1, tn), lambda l, j: (l, 0, j)),
        out_shape=jax.ShapeDtypeStruct((depth, 1, n), F32),
        compiler_params=_cparams(("arbitrary", "arbitrary")),
        name="ada_mod",
    )(c.reshape(d, 1), ada_w, ada_b.reshape(depth, 1, n))


def _ffn_kernel(x_ref, sh_ref, sc_ref, g_ref, w1_ref, w3_ref, w2_ref, o_ref, h_scr, acc_scr):
    f = pl.program_id(1)

    @pl.when(f == 0)
    def _():
        h_scr[...] = _modulate(x_ref[...], sh_ref[...], sc_ref[...]).astype(BF16)
        acc_scr[...] = jnp.zeros_like(acc_scr)

    h = h_scr[...]
    a = _dot(h, w1_ref[...])
    b = _dot(h, w3_ref[...])
    u = (a * _sigmoid(a) * b).astype(BF16)
    acc_scr[...] += _dot(u, w2_ref[...])

    @pl.when(f == pl.num_programs(1) - 1)
    def _():
        o_ref[...] = x_ref[...] + g_ref[...] * acc_scr[...]


def _ffn(x, shift, scale, gate, w1, w3, w2):
    t, d = x.shape
    dff = w1.shape[1]
    tm, tf = 512, 512
    row = lambda i, f: (0, 0)
    return pl.pallas_call(
        _ffn_kernel,
        grid=(t // tm, dff // tf),
        in_specs=[
            pl.BlockSpec((tm, d), lambda i, f: (i, 0)),
            pl.BlockSpec((1, d), row),
            pl.BlockSpec((1, d), row),
            pl.BlockSpec((1, d), row),
            pl.BlockSpec((d, tf), lambda i, f: (0, f)),
            pl.BlockSpec((d, tf), lambda i, f: (0, f)),
            pl.BlockSpec((tf, d), lambda i, f: (f, 0)),
        ],
        out_specs=pl.BlockSpec((tm, d), lambda i, f: (i, 0)),
        out_shape=jax.ShapeDtypeStruct((t, d), F32),
        scratch_shapes=[pltpu.VMEM((tm, d), BF16), pltpu.VMEM((tm, d), F32)],
        compiler_params=_cparams(("arbitrary", "arbitrary")),
        name="ffn",
    )(x, shift, scale, gate, w1, w3, w2)


def _proj_res_kernel(a_ref, w_ref, x_ref, g_ref, o_ref):
    o_ref[...] = x_ref[...] + g_ref[...] * _dot(a_ref[...], w_ref[...])


def _proj_res(a, w, x, gate):
    t, k = a.shape
    n = w.shape[1]
    tm, tn = 1024, 512
    return pl.pallas_call(
        _proj_res_kernel,
        grid=(t // tm, n // tn),
        in_specs=[
            pl.BlockSpec((tm, k), lambda i, j: (i, 0)),
            pl.BlockSpec((k, tn), lambda i, j: (0, j)),
            pl.BlockSpec((tm, tn), lambda i, j: (i, j)),
            pl.BlockSpec((1, tn), lambda i, j: (0, j)),
        ],
        out_specs=pl.BlockSpec((tm, tn), lambda i, j: (i, j)),
        out_shape=jax.ShapeDtypeStruct((t, n), F32),
        compiler_params=_cparams(("arbitrary", "arbitrary")),
        name="proj_res",
    )(a, w, x, gate)


def _norm_proj_kernel(x_ref, sh_ref, sc_ref, w_ref, o_ref, h_scr):
    @pl.when(pl.program_id(1) == 0)
    def _():
        h_scr[...] = _modulate(x_ref[...], sh_ref[...], sc_ref[...]).astype(BF16)

    o_ref[...] = _dot(h_scr[...], w_ref[...])


def _norm_proj(x, shift, scale, w, tn):
    t, d = x.shape
    n = w.shape[1]
    tm = 512
    row = lambda i, j: (0, 0)
    return pl.pallas_call(
        _norm_proj_kernel,
        grid=(t // tm, n // tn),
        in_specs=[
            pl.BlockSpec((tm, d), lambda i, j: (i, 0)),
            pl.BlockSpec((1, d), row),
            pl.BlockSpec((1, d), row),
            pl.BlockSpec((d, tn), lambda i, j: (0, j)),
        ],
        out_specs=pl.BlockSpec((tm, tn), lambda i, j: (i, j)),
        out_shape=jax.ShapeDtypeStruct((t, n), F32),
        scratch_shapes=[pltpu.VMEM((tm, d), BF16)],
        compiler_params=_cparams(("arbitrary", "arbitrary")),
        name="norm_proj",
    )(x, shift, scale, w)


def _bmm_kernel(a_ref, w_ref, o_ref):
    o_ref[0] = _dot(a_ref[0], w_ref[0])


def _bmm3(a, w):
    nj, k, n = w.shape
    t = a.shape[1]
    tm, tn = 1024, 512
    return pl.pallas_call(
        _bmm_kernel,
        grid=(nj, t // tm, n // tn),
        in_specs=[
            pl.BlockSpec((1, tm, k), lambda j, i, c: (j, i, 0)),
            pl.BlockSpec((1, k, tn), lambda j, i, c: (j, 0, c)),
        ],
        out_specs=pl.BlockSpec((1, tm, tn), lambda j, i, c: (j, i, c)),
        out_shape=jax.ShapeDtypeStruct((nj, t, n), F32),
        compiler_params=_cparams(("arbitrary", "arbitrary", "arbitrary")),
        name="rkv_proj",
    )(a, w)


def _lora_kernel(act, a_ref, w1_ref, w2_ref, o_ref):
    z = _dot(a_ref[0], w1_ref[...])
    if act == "tanh":
        z = jnp.tanh(z)
    elif act == "sigmoid":
        z = _sigmoid(z)
    o_ref[...] = _dot(z.astype(BF16), w2_ref[...])


def _lora(mixes, slab, w1, w2, act):
    _, t, k = mixes.shape
    r = w1.shape[1]
    rp = -(-r // LANES) * LANES
    w1p = jnp.pad(w1, ((0, 0), (0, rp - r)))
    w2p = jnp.pad(w2, ((0, rp - r), (0, 0)))
    n = w2.shape[1]
    tm = 512
    return pl.pallas_call(
        functools.partial(_lora_kernel, act),
        grid=(t // tm,),
        in_specs=[
            pl.BlockSpec((1, tm, k), lambda i: (slab, i, 0)),
            pl.BlockSpec((k, rp), lambda i: (0, 0)),
            pl.BlockSpec((rp, n), lambda i: (0, 0)),
        ],
        out_specs=pl.BlockSpec((tm, n), lambda i: (i, 0)),
        out_shape=jax.ShapeDtypeStruct((t, n), F32),
        compiler_params=_cparams(("arbitrary",)),
        name="lora_" + act,
    )(mixes, w1p, w2p)


def _rwkv_prep_kernel(x_ref, sh_ref, sc_ref, mu_ref, o_ref, carry):
    i = pl.program_id(0)
    tm = x_ref.shape[0]

    @pl.when(i == 0)
    def _():
        carry[...] = jnp.zeros_like(carry)

    h = _modulate(x_ref[...], sh_ref[...], sc_ref[...])
    prev_last = carry[7:8, :]
    rolled = pltpu.roll(h, 1, axis=0)
    rows = lax.broadcasted_iota(I32, h.shape, 0)
    h_prev = jnp.where(rows == 0, prev_last, rolled)
    carry[...] = h[tm - 8:tm, :]
    dx = h_prev - h
    for j in range(6):
        o_ref[j] = (h + dx * mu_ref[j:j + 1, :]).astype(BF16)


def _rwkv_prep(x, shift, scale, mu):
    t, d = x.shape
    tm = 256
    return pl.pallas_call(
        _rwkv_prep_kernel,
        grid=(t // tm,),
        in_specs=[
            pl.BlockSpec((tm, d), lambda i: (i, 0)),
            pl.BlockSpec((1, d), lambda i: (0, 0)),
            pl.BlockSpec((1, d), lambda i: (0, 0)),
            pl.BlockSpec((6, d), lambda i: (0, 0)),
        ],
        out_specs=pl.BlockSpec((6, tm, d), lambda i: (0, i, 0)),
        out_shape=jax.ShapeDtypeStruct((6, t, d), BF16),
        scratch_shapes=[pltpu.VMEM((8, d), F32)],
        compiler_params=_cparams(("arbitrary",)),
        name="rwkv_prep",
    )(x, shift, scale, mu)


def _split2(x):
    hi = x.astype(BF16)
    lo = (x - hi.astype(F32)).astype(BF16)
    return hi, lo


def _rwkv_chunk_kernel(has_vres, n_chunks, *refs):
    if has_vres:
        (r_ref, k_ref, v_ref, lw_ref, la_ref, lg_ref, lv_ref, vf_ref,
         w0_ref, a0_ref, kk_ref, ka_ref, rk_ref, lnw_ref, lnb_ref, v0_ref,
         o_ref, s_scr) = refs
    else:
        (r_ref, k_ref, v_ref, lw_ref, la_ref, lg_ref,
         w0_ref, a0_ref, kk_ref, ka_ref, rk_ref, lnw_ref, lnb_ref,
         o_ref, s_scr) = refs

    L = CHUNK
    W = LANES

    @pl.when(pl.program_id(1) == 0)
    def _():
        s_scr[...] = jnp.zeros_like(s_scr)

    lane = lax.broadcasted_iota(I32, (1, W), 1)
    head_masks = [(lane < A_HEAD).astype(F32), (lane >= A_HEAD).astype(F32)]
    rr = lax.broadcasted_iota(I32, (L, L), 0)
    cc = lax.broadcasted_iota(I32, (L, L), 1)
    low_strict = rr > cc
    low_incl = rr >= cc
    eye_l = (rr == cc).astype(F32)
    tri_ones = low_incl.astype(BF16)
    diag_blk = (rr >> 3) == (cc >> 3)
    merge_blks = [((rr >> (s + 1)) == (cc >> (s + 1))) & ((rr >> s) > (cc >> s)) for s in (3, 4, 5)]
    r2 =lax.broadcasted_iota(I32, (W, W), 0)
    c2 = lax.broadcasted_iota(I32, (W, W), 1)
    same_head = (r2 // A_HEAD) == (c2 // A_HEAD)
    seg_ones = same_head.astype(BF16)

    def segsum(x):
        hi, lo = _split2(x)
        return _dot(hi, seg_ones) + _dot(lo, seg_ones)

    w0 = w0_ref[...]
    a0 = a0_ref[...]
    k_k = kk_ref[...]
    k_a = ka_ref[...]
    r_k = rk_ref[...]
    ln_w = lnw_ref[...]
    ln_b = lnb_ref[...]

    for c in range(n_chunks):
        rows = slice(c * L, (c + 1) * L)
        r = r_ref[0, rows, :]
        k = k_ref[0, rows, :]
        v = v_ref[0, rows, :]
        z = -(w0 + lw_ref[rows, :])
        softplus = jnp.maximum(z, 0.0) + jnp.log(1.0 + jnp.exp(-jnp.abs(z)))
        logw = -jnp.exp(-softplus - 0.5)
        a = _sigmoid(a0 + la_ref[rows, :])
        if has_vres:
            v = v + (vf_ref[rows, :] - v) * _sigmoid(v0_ref[...] + lv_ref[rows, :])
        g = lg_ref[rows, :]
        kk = k * k_k
        kk = kk / jnp.maximum(jnp.sqrt(segsum(kk * kk)), 1e-12)
        k = k * (1.0 + (a - 1.0) * k_a)

        h1 = logw.astype(BF16)
        rem = logw - h1.astype(F32)
        h2 = rem.astype(BF16)
        h3 = (rem - h2.astype(F32)).astype(BF16)
        cum = _dot(tri_ones, h1) + _dot(tri_ones, h2) + _dot(tri_ones, h3)
        w_in = jnp.exp(cum)
        w_prev = jnp.exp(cum - logw)
        w_inv = jnp.exp(-cum)
        w_last = w_in[L - 1:L, :]

        a_t = -kk * w_prev
        b_t = kk * a * w_inv
        k_t = k * w_inv
        r_t = r * w_in
        b_bf = b_t.astype(BF16)
        k_bf = k_t.astype(BF16)

        p_sum = jnp.zeros((L, W), F32)
        q_sum = jnp.zeros((L, W), F32)
        rp = r_t
        y0 = jnp.zeros((L, W), F32)
        for hm in head_masks:
            a_e = (a_t * hm).astype(BF16)
            r_e = (r_t * hm).astype(BF16)
            v_e = (v * hm).astype(BF16)
            x_ab = jnp.where(low_strict, _dot_nt(a_e, b_bf), 0.0)
            x_ak = jnp.where(low_strict, _dot_nt(a_e, k_bf), 0.0)
            x_rb = jnp.where(low_incl, _dot_nt(r_e, b_bf), 0.0).astype(BF16)
            x_rk = jnp.where(low_incl, _dot_nt(r_e, k_bf), 0.0).astype(BF16)
            xp = jnp.where(diag_blk, x_ab, 0.0)
            tm = eye_l + xp
            for _ in range(2):
                xb = xp.astype(BF16)
                xp = _dot(xb, xb)
                tm = tm + _dot(tm.astype(BF16), xp.astype(BF16))
            for off_blk in merge_blks:
                tb = tm.astype(BF16)
                x_off = jnp.where(off_blk, x_ab, 0.0).astype(BF16)
                tm = tm + _dot(_dot(tb, x_off).astype(BF16), tb)
            tb = tm.astype(BF16)
            p_e = _dot(tb, a_e)
            q_e = _dot(_dot(tb, x_ak.astype(BF16)).astype(BF16), v_e)
            p_eb = p_e.astype(BF16)
            q_eb = q_e.astype(BF16)
            rp = rp + _dot(x_rb, p_eb)
            y0 = y0 + _dot(x_rb, q_eb) + _dot(x_rk, v_e)
            p_sum = p_sum + p_e
            q_sum = q_sum + q_e

        s = s_scr[...]
        s_hi, s_lo = _split2(s)
        rp_b = rp.astype(BF16)
        y = _dot_nt(rp_b, s_hi) + _dot_nt(rp_b, s_lo) + y0

        m_lr = (jnp.where(same_head, _dot_tn(p_sum.astype(BF16), b_bf), 0.0) * w_last).astype(BF16)
        qv = jnp.concatenate([q_sum, v], axis=0).astype(BF16)
        bk = jnp.concatenate([b_bf, k_bf], axis=0)
        c_bd = jnp.where(same_head, _dot_tn(qv, bk), 0.0) * w_last
        s_scr[...] = s * w_last + _dot(s_hi, m_lr) + _dot(s_lo, m_lr) + c_bd

        mean = segsum(y) * (1.0 / A_HEAD)
        dlt = y - mean
        var = segsum(dlt * dlt) * (1.0 / A_HEAD)
        yn = dlt * lax.rsqrt(var + A_GN_EPS) * ln_w + ln_b
        bonus = segsum(r * k * r_k) * v
        o_ref[rows, :] = ((yn + bonus) * g).astype(BF16)


def _rwkv_chunks(rkv, lw, la, lg, lv, v_first, w0, a0, k_k, k_a, r_k, ln_w, ln_b, v0):
    _, t, d = rkv.shape
    n_chunks = 4
    tm = n_chunks * CHUNK
    has_vres = lv is not None
    blk = lambda p, c: (c, p)
    prm = lambda p, c: (0, p)
    slab = lambda j: pl.BlockSpec((1, tm, LANES), lambda p, c: (j, c, p))
    in_specs = [slab(0), slab(1), slab(2)] + [pl.BlockSpec((tm, LANES), blk)] * (5 if has_vres else 3)
    in_specs += [pl.BlockSpec((1, LANES), prm)] * (8 if has_vres else 7)
    args = [rkv, rkv, rkv, lw, la, lg]
    if has_vres:
        args += [lv, v_first]
    args += [w0, a0, k_k, k_a, r_k, ln_w, ln_b]
    if has_vres:
        args += [v0]
    return pl.pallas_call(
        functools.partial(_rwkv_chunk_kernel, has_vres, n_chunks),
        grid=(d // LANES, t // tm),
        in_specs=in_specs,
        out_specs=pl.BlockSpec((tm, LANES), blk),
        out_shape=jax.ShapeDtypeStruct((t, d), BF16),
        scratch_shapes=[pltpu.VMEM((LANES, LANES), F32)],
        compiler_params=_cparams(("arbitrary", "arbitrary")),
        name="rwkv_chunks",
    )(*args)


def _dsa_split_kernel(p_ref, kvn_ref, kin_ref, q_ref, qi_ref, ckv_ref, kidx_ref, wi_ref):
    tm = p_ref.shape[0]
    nb = tm // Q_BLOCK
    o = 0
    for b in range(nb):
        rows = slice(b * Q_BLOCK, (b + 1) * Q_BLOCK)
        for h in range(B_HEADS):
            q_ref[b, h] = p_ref[rows, o + h * B_HEAD_DIM:o + (h + 1) * B_HEAD_DIM].astype(BF16)
    o += B_Q_W
    ckv = p_ref[:, o:o + B_KV_RANK]
    ckv = ckv * lax.rsqrt(jnp.mean(ckv * ckv, axis=-1, keepdims=True) + RMS_EPS) * kvn_ref[...]
    ckv_ref[...] = ckv.astype(BF16)
    o += B_KV_RANK
    for b in range(nb):
        rows = slice(b * Q_BLOCK, (b + 1) * Q_BLOCK)
        for h in range(B_IDX_HEADS):
            qi_ref[b, h] = p_ref[rows, o + h * B_IDX_DIM:o + (h + 1) * B_IDX_DIM].astype(BF16)
    o += B_QI_W
    kidx = p_ref[:, o:o + B_IDX_DIM]
    kidx = kidx * lax.rsqrt(jnp.mean(kidx * kidx, axis=-1, keepdims=True) + RMS_EPS) * kin_ref[...]
    kidx_ref[...] = kidx.astype(BF16)
    o += B_IDX_DIM
    wi_ref[...] = p_ref[:, o:o + LANES] * B_IDX_W_SCALE


def _dsa_split(proj, kv_norm, kidx_norm):
    t, n = proj.shape
    tm = 256
    nb = tm // Q_BLOCK
    return pl.pallas_call(
        _dsa_split_kernel,
        grid=(t // tm,),
        in_specs=[
            pl.BlockSpec((tm, n), lambda i: (i, 0)),
            pl.BlockSpec((1, B_KV_RANK), lambda i: (0, 0)),
            pl.BlockSpec((1, B_IDX_DIM), lambda i: (0, 0)),
        ],
        out_specs=[
            pl.BlockSpec((nb, B_HEADS, Q_BLOCK, B_HEAD_DIM), lambda i: (i, 0, 0, 0)),
            pl.BlockSpec((nb, B_IDX_HEADS, Q_BLOCK, B_IDX_DIM), lambda i: (i, 0, 0, 0)),
            pl.BlockSpec((tm, B_KV_RANK), lambda i: (i, 0)),
            pl.BlockSpec((tm, B_IDX_DIM), lambda i: (i, 0)),
            pl.BlockSpec((tm, LANES), lambda i: (i, 0)),
        ],
        out_shape=[
            jax.ShapeDtypeStruct((t // Q_BLOCK, B_HEADS, Q_BLOCK, B_HEAD_DIM), BF16),
            jax.ShapeDtypeStruct((t // Q_BLOCK, B_IDX_HEADS, Q_BLOCK, B_IDX_DIM), BF16),
            jax.ShapeDtypeStruct((t, B_KV_RANK), BF16),
            jax.ShapeDtypeStruct((t, B_IDX_DIM), BF16),
            jax.ShapeDtypeStruct((t, LANES), F32),
        ],
        compiler_params=_cparams(("arbitrary",)),
        name="dsa_split",
    )(proj, kv_norm, kidx_norm)


def _dsa_attn_kernel(topk, tk, idx_bits,
                     q_ref, qi_ref, wcol_ref, kidx_ref, ckv_ref, wuk_ref, wuv_ref, o_ref,
                     key_scr, qlat_scr, acc_scr, m_scr, l_scr, j0_scr):
    i = pl.program_id(0)
    QB = Q_BLOCK
    n_sub = tk // LANES
    n_kt = (i * QB + QB + tk - 1) // tk
    qpos = i * QB + lax.broadcasted_iota(I32, (QB, 1), 0)
    limit = (qpos // CHUNK + 1) * CHUNK

    for h in range(B_HEADS):
        ql = _dot(q_ref[0, h], wuk_ref[h]) * B_SCALE
        qlat_scr[h * QB:(h + 1) * QB, :] = ql.astype(BF16)

    qi = qi_ref[0].reshape(B_IDX_HEADS * QB, B_IDX_DIM)
    wb = jnp.broadcast_to(wcol_ref[0], (B_IDX_HEADS * QB, LANES))

    def score_body(kt, carry):
        off = pl.multiple_of(kt * tk, tk)
        kk = kidx_ref[pl.ds(off, tk), :]
        rel = jnp.maximum(_dot_nt(qi, kk), 0.0)
        for s in range(n_sub):
            part = rel[:, s * LANES:(s + 1) * LANES] * wb
            sc = jnp.sum(part.reshape(B_IDX_HEADS, QB, LANES), axis=0)
            sc = jnp.where(sc == 0.0, 0.0, sc)
            bits = pltpu.bitcast(sc, I32)
            key = bits ^ ((bits >> 31) & INT_MAX)
            kpos = off + s * LANES + lax.broadcasted_iota(I32, (1, LANES), 1)
            key = jnp.where(kpos < limit, key, INT_MIN)
            key_scr[:, pl.ds(pl.multiple_of(off + s * LANES, LANES), LANES)] = key
        return carry

    lax.fori_loop(0, n_kt, score_body, 0)

    def count_rows(pred):
        def body(kt, acc):
            off = pl.multiple_of(kt * tk, tk)
            tile = key_scr[:, pl.ds(off, tk)]
            for s in range(n_sub):
                kpos = off + s * LANES + lax.broadcasted_iota(I32, (1, LANES), 1)
                acc = acc + jnp.where(pred(tile[:, s * LANES:(s + 1) * LANES], kpos), 1.0, 0.0)
            return acc
        acc = lax.fori_loop(0, n_kt, body, jnp.zeros((QB, LANES), F32))
        return jnp.sum(acc, axis=1, keepdims=True)

    def bit_body(b, thr_u):
        cand = thr_u | jnp.left_shift(jnp.int32(1), 31 - b)
        cand_b = jnp.broadcast_to(cand ^ INT_MIN, (QB, LANES))
        cnt = count_rows(lambda key, kpos: key >= cand_b)
        return jnp.where(cnt >= topk, cand, thr_u)

    thr = lax.fori_loop(0, 32, bit_body, jnp.zeros((QB, 1), I32)) ^ INT_MIN
    thr_b = jnp.broadcast_to(thr, (QB, LANES))
    cnt_gt = count_rows(lambda key, kpos: key > thr_b)
    cnt_ge = count_rows(lambda key, kpos: key >= thr_b)
    need = topk - cnt_gt

    j0_scr[...] = jnp.full((QB, 1), INT_MAX, I32)
    excess = jnp.max(jnp.where(thr > INT_MIN, cnt_ge, 0.0)) > topk

    @pl.when(excess)
    def _():
        def tie_body(b, j0):
            cand = j0 | jnp.left_shift(jnp.int32(1), idx_bits - 1 - b)
            cand_b = jnp.broadcast_to(cand, (QB, LANES))
            cnt = count_rows(lambda key, kpos: (key == thr_b) & (kpos < cand_b))
            return jnp.where(cnt < need, cand, j0)
        j0_scr[...] = lax.fori_loop(0, idx_bits, tie_body, jnp.zeros((QB, 1), I32))

    j0_b = jnp.broadcast_to(j0_scr[...], (QB, LANES))

    m_scr[...] = jnp.full(m_scr.shape, NEG_BIG, F32)
    l_scr[...] = jnp.zeros_like(l_scr)
    acc_scr[...] = jnp.zeros_like(acc_scr)

    def att_body(kt, carry):
        off = pl.multiple_of(kt * tk, tk)
        tile = key_scr[:, pl.ds(off, tk)]
        biases = []
        for s in range(n_sub):
            key = tile[:, s * LANES:(s + 1) * LANES]
            kpos = off + s * LANES + lax.broadcasted_iota(I32, (1, LANES), 1)
            sel = ((key > thr_b) | ((key == thr_b) & (kpos <= j0_b))) & (key > INT_MIN)
            biases.append(jnp.where(sel, 0.0, NEG_BIG))
        bias = jnp.concatenate(biases, axis=1)
        ckv_t = ckv_ref[pl.ds(off, tk), :]
        s3 = _dot_nt(qlat_scr[...], ckv_t).reshape(B_HEADS, QB, tk) + bias[None]
        s2 = s3.reshape(B_HEADS * QB, tk)
        m_old = m_scr[...]
        m_new = jnp.maximum(m_old, jnp.max(s2, axis=1, keepdims=True))
        alpha = jnp.exp(m_old - m_new)
        p = jnp.exp(s2 - m_new)
        l_scr[...] = alpha * l_scr[...] + jnp.sum(p, axis=1, keepdims=True)
        acc_scr[...] = alpha * acc_scr[...] + _dot(p.astype(BF16), ckv_t)
        m_scr[...] = m_new
        return carry

    lax.fori_loop(0, n_kt, att_body, 0)

    o_lat = (acc_scr[...] / l_scr[...]).astype(BF16)
    for h in range(B_HEADS):
        oh = _dot(o_lat[h * QB:(h + 1) * QB, :], wuv_ref[h])
        o_ref[:, h * B_HEAD_DIM:(h + 1) * B_HEAD_DIM] = oh.astype(BF16)


def _dsa_attn(q, qi, wcol, kidx, ckv, w_uk, w_uv):
    nb = q.shape[0]
    t = kidx.shape[0]
    tk = 256
    topk = min(TOPK_MAX, t // 4)
    idx_bits = max(1, (t - 1).bit_length())
    const2 = lambda i: (0, 0)
    const3 = lambda i: (0, 0, 0)
    hq = B_HEADS * Q_BLOCK
    return pl.pallas_call(
        functools.partial(_dsa_attn_kernel, topk, tk, idx_bits),
        grid=(nb,),
        in_specs=[
            pl.BlockSpec((1, B_HEADS, Q_BLOCK, B_HEAD_DIM), lambda i: (i, 0, 0, 0)),
            pl.BlockSpec((1, B_IDX_HEADS, Q_BLOCK, B_IDX_DIM), lambda i: (i, 0, 0, 0)),
            pl.BlockSpec((1, B_IDX_HEADS * Q_BLOCK, 1), lambda i: (i, 0, 0)),
            pl.BlockSpec((t, B_IDX_DIM), const2, pipeline_mode=pl.Buffered(1)),
            pl.BlockSpec((t, B_KV_RANK), const2, pipeline_mode=pl.Buffered(1)),
            pl.BlockSpec((B_HEADS, B_HEAD_DIM, B_KV_RANK), const3, pipeline_mode=pl.Buffered(1)),
            pl.BlockSpec((B_HEADS, B_KV_RANK, B_HEAD_DIM), const3, pipeline_mode=pl.Buffered(1)),
        ],
        out_specs=pl.BlockSpec((Q_BLOCK, B_Q_W), lambda i: (i, 0)),
        out_shape=jax.ShapeDtypeStruct((t, B_Q_W), BF16),
        scratch_shapes=[
            pltpu.VMEM((Q_BLOCK, t), I32),
            pltpu.VMEM((hq, B_KV_RANK), BF16),
            pltpu.VMEM((hq, B_KV_RANK), F32),
            pltpu.VMEM((hq, 1), F32),
            pltpu.VMEM((hq, 1), F32),
            pltpu.VMEM((Q_BLOCK, 1), I32),
        ],
        compiler_params=_cparams(("arbitrary",)),
        name="dsa_attn",
    )(q, qi, wcol, kidx, ckv, w_uk, w_uv)


def _final_norm_kernel(x_ref, w_ref, o_ref):
    x = x_ref[...]
    o_ref[...] = x * lax.rsqrt(jnp.mean(x * x, axis=-1, keepdims=True) + RMS_EPS) * w_ref[...]


def _final_norm(x, w):
    t, d = x.shape
    tm = 512
    return pl.pallas_call(
        _final_norm_kernel,
        grid=(t // tm,),
        in_specs=[pl.BlockSpec((tm, d), lambda i: (i, 0)), pl.BlockSpec((1, d), lambda i: (0, 0))],
        out_specs=pl.BlockSpec((tm, d), lambda i: (i, 0)),
        out_shape=jax.ShapeDtypeStruct((t, d), F32),
        compiler_params=_cparams(("arbitrary",)),
        name="final_norm",
    )(x, w)


def _row(v):
    return v.reshape(1, -1)


def _rwkv_layer(x, shift, scale, gate, v_first, mu, w_rkv, w0, w1, w2, a0, a1, a2,
                g1, g2, k_k, k_a, r_k, ln_w, ln_b, w_o, vres):
    mixes = _rwkv_prep(x, shift, scale, mu[jnp.array([0, 2, 3, 1, 4, 5])])
    rkv = _bmm3(mixes, w_rkv.astype(BF16))
    lw = _lora(mixes, 3, w1.astype(BF16), w2.astype(BF16), "tanh")
    la = _lora(mixes, 4, a1.astype(BF16), a2.astype(BF16), "none")
    lg = _lora(mixes, 5, g1.astype(BF16), g2.astype(BF16), "sigmoid")
    if vres is None:
        lv, v0 = None, None
        v_first = rkv[2]
    else:
        v0, v1, v2 = vres
        lv = _lora(mixes, 2, v1.astype(BF16), v2.astype(BF16), "none")
        v0 = _row(v0)
    y = _rwkv_chunks(rkv, lw, la, lg, lv, v_first, _row(w0), _row(a0), _row(k_k), _row(k_a),
                     _row(r_k), _row(ln_w), _row(ln_b), v0)
    return _proj_res(y, w_o.astype(BF16), x, gate), v_first


def _dsa_layer(x, shift, scale, gate, w_in, kv_norm, kidx_norm, w_uk, w_uv, w_o):
    t = x.shape[0]
    n_in = w_in.shape[1]
    n_pad = -(-(n_in - B_IDX_HEADS + LANES) // 768) * 768
    w_in_p = jnp.pad(w_in, ((0, 0), (0, n_pad - n_in))).astype(BF16)
    proj = _norm_proj(x, shift, scale, w_in_p, 768)
    q, qi, ckv, kidx, wi = _dsa_split(proj, _row(kv_norm), _row(kidx_norm))
    nb = t // Q_BLOCK
    wcol = wi[:, :B_IDX_HEADS].reshape(nb, Q_BLOCK, B_IDX_HEADS).transpose(0, 2, 1)
    wcol = wcol.reshape(nb, B_IDX_HEADS * Q_BLOCK, 1)
    o = _dsa_attn(q, qi, wcol, kidx, ckv, w_uk.astype(BF16), w_uv.astype(BF16))
    return _proj_res(o, w_o.astype(BF16), x, gate)


def kernel(x, c, ada_w, ada_b, a_mu, a_w_rkv, a_w0, a_w1, a_w2, a_a0, a_a1, a_a2, a_v0, a_v1, a_v2, a_g1, a_g2, a_k_k, a_k_a, a_r_k, a_ln_w, a_ln_b, a_w_o, b_w_in, b_kv_norm, b_kidx_norm, b_w_uk, b_w_uv, b_w_o, f_w1, f_w3, f_w2, final_norm):
    b, t, d = x.shape
    assert b == 1 and d == D_MODEL
    mod_all = _ada(c, ada_w, ada_b)
    xs = x.reshape(t, d)
    v_first = None
    for i in range(DEPTH):
        shift1, scale1, gate1, shift2, scale2, gate2 = (
            mod_all[i, :, s * d:(s + 1) * d] for s in range(6))
        j = i // 2
        if i % 2 == 0:
            vres = None if j == 0 else (a_v0[j - 1], a_v1[j - 1], a_v2[j - 1])
            xs, v_first = _rwkv_layer(
                xs, shift1, scale1, gate1, v_first, a_mu[j], a_w_rkv[j], a_w0[j], a_w1[j], a_w2[j],
                a_a0[j], a_a1[j], a_a2[j], a_g1[j], a_g2[j], a_k_k[j], a_k_a[j], a_r_k[j],
                a_ln_w[j], a_ln_b[j], a_w_o[j], vres)
        else:
            xs = _dsa_layer(xs, shift1, scale1, gate1, b_w_in[j], b_kv_norm[j], b_kidx_norm[j],
                            b_w_uk[j], b_w_uv[j], b_w_o[j])
        xs = _ffn(xs, shift2, scale2, gate2, f_w1[i].astype(BF16), f_w3[i].astype(BF16),
                  f_w2[i].astype(BF16))
    return _final_norm(xs, _row(final_norm)).reshape(b, t, d)
```

```python
import functools

import jax
import jax.numpy as jnp
from jax import lax
from jax.experimental import pallas as pl
from jax.experimental.pallas import tpu as pltpu

F32 = jnp.float32
BF16 = jnp.bfloat16
I32 = jnp.int32

D_MODEL = 2048
DEPTH = 4
CHUNK = 64
RMS_EPS = 1e-6
A_HEAD = 64
A_HEADS = D_MODEL // A_HEAD
A_GN_EPS = A_HEAD * 1e-5
B_HEADS = 16
B_HEAD_DIM = 128
B_KV_RANK = 512
B_IDX_HEADS = 8
B_IDX_DIM = 128
TOPK_MAX = 256
Q_BLOCK = 128
B_Q_W = B_HEADS * B_HEAD_DIM
B_QI_W = B_IDX_HEADS * B_IDX_DIM
B_SCALE = B_HEAD_DIM ** -0.5
B_IDX_W_SCALE = (B_IDX_HEADS ** -0.5) * (B_IDX_DIM ** -0.5)
D_FF = 5632

LANES = 128
VMEM_LIMIT = 56 * 1024 * 1024
INT_MIN = -(2 ** 31)
INT_MAX = 2 ** 31 - 1
NEG_BIG = -1e30

_NT = (((1,), (1,)), ((), ()))
_TN = (((0,), (0,)), ((), ()))


def _cparams(sem):
    return pltpu.CompilerParams(dimension_semantics=sem, vmem_limit_bytes=VMEM_LIMIT)


def _dot(a, b):
    return jnp.dot(a, b, preferred_element_type=F32)


def _dot_nt(a, b):
    return lax.dot_general(a, b, _NT, preferred_element_type=F32)


def _dot_tn(a, b):
    return lax.dot_general(a, b, _TN, preferred_element_type=F32)


def _modulate(x, shift, scale):
    ms = jnp.mean(x * x, axis=-1, keepdims=True)
    return x * lax.rsqrt(ms + RMS_EPS) * (1.0 + scale) + shift


def _sigmoid(x):
    return 1.0 / (1.0 + jnp.exp(-x))


def _ada_kernel(c_ref, w_ref, b_ref, o_ref):
    c = c_ref[...]
    s = c * _sigmoid(c)
    o_ref[0] = jnp.sum(s * w_ref[0], axis=0, keepdims=True) + b_ref[0]


def _ada(c, ada_w, ada_b):
    depth, d, n = ada_w.shape
    tn = 1024
    return pl.pallas_call(
        _ada_kernel,
        grid=(depth, n // tn),
        in_specs=[
            pl.BlockSpec((d, 1), lambda l, j: (0, 0)),
            pl.BlockSpec((1, d, tn), lambda l, j: (l, 0, j)),
            pl.BlockSpec((1, 1, tn), lambda l, j: (l, 0, j)),
        ],
        out_specs=pl.BlockSpec((1, 1, tn), lambda l, j: (l, 0, j)),
        out_shape=jax.ShapeDtypeStruct((depth, 1, n), F32),
        compiler_params=_cparams(("arbitrary", "arbitrary")),
        name="ada_mod",
    )(c.reshape(d, 1), ada_w, ada_b.reshape(depth, 1, n))


def _ffn_kernel(x_ref, sh_ref, sc_ref, g_ref, w1_ref, w3_ref, w2_ref, o_ref, h_scr, acc_scr):
    f = pl.program_id(1)

    @pl.when(f == 0)
    def _():
        h_scr[...] = _modulate(x_ref[...], sh_ref[...], sc_ref[...]).astype(BF16)
        acc_scr[...] = jnp.zeros_like(acc_scr)

    h = h_scr[...]
    a = _dot(h, w1_ref[...])
    b = _dot(h, w3_ref[...])
    u = (a * _sigmoid(a) * b).astype(BF16)
    acc_scr[...] += _dot(u, w2_ref[...])

    @pl.when(f == pl.num_programs(1) - 1)
    def _():
        o_ref[...] = x_ref[...] + g_ref[...] * acc_scr[...]


def _ffn(x, shift, scale, gate, w1, w3, w2):
    t, d = x.shape
    dff = w1.shape[1]
    tm, tf = 512, 512
    row = lambda i, f: (0, 0)
    return pl.pallas_call(
        _ffn_kernel,
        grid=(t // tm, dff // tf),
        in_specs=[
            pl.BlockSpec((tm, d), lambda i, f: (i, 0)),
            pl.BlockSpec((1, d), row),
            pl.BlockSpec((1, d), row),
            pl.BlockSpec((1, d), row),
            pl.BlockSpec((d, tf), lambda i, f: (0, f)),
            pl.BlockSpec((d, tf), lambda i, f: (0, f)),
            pl.BlockSpec((tf, d), lambda i, f: (f, 0)),
        ],
        out_specs=pl.BlockSpec((tm, d), lambda i, f: (i, 0)),
        out_shape=jax.ShapeDtypeStruct((t, d), F32),
        scratch_shapes=[pltpu.VMEM((tm, d), BF16), pltpu.VMEM((tm, d), F32)],
        compiler_params=_cparams(("arbitrary", "arbitrary")),
        name="ffn",
    )(x, shift, scale, gate, w1, w3, w2)


def _proj_res_kernel(a_ref, w_ref, x_ref, g_ref, o_ref):
    o_ref[...] = x_ref[...] + g_ref[...] * _dot(a_ref[...], w_ref[...])


def _proj_res(a, w, x, gate):
    t, k = a.shape
    n = w.shape[1]
    tm, tn = 1024, 512
    return pl.pallas_call(
        _proj_res_kernel,
        grid=(t // tm, n // tn),
        in_specs=[
            pl.BlockSpec((tm, k), lambda i, j: (i, 0)),
            pl.BlockSpec((k, tn), lambda i, j: (0, j)),
            pl.BlockSpec((tm, tn), lambda i, j: (i, j)),
            pl.BlockSpec((1, tn), lambda i, j: (0, j)),
        ],
        out_specs=pl.BlockSpec((tm, tn), lambda i, j: (i, j)),
        out_shape=jax.ShapeDtypeStruct((t, n), F32),
        compiler_params=_cparams(("arbitrary", "arbitrary")),
        name="proj_res",
    )(a, w, x, gate)


def _norm_proj_kernel(x_ref, sh_ref, sc_ref, w_ref, o_ref, h_scr):
    @pl.when(pl.program_id(1) == 0)
    def _():
        h_scr[...] = _modulate(x_ref[...], sh_ref[...], sc_ref[...]).astype(BF16)

    o_ref[...] = _dot(h_scr[...], w_ref[...])


def _norm_proj(x, shift, scale, w, tn):
    t, d = x.shape
    n = w.shape[1]
    tm = 512
    row = lambda i, j: (0, 0)
    return pl.pallas_call(
        _norm_proj_kernel,
        grid=(t // tm, n // tn),
        in_specs=[
            pl.BlockSpec((tm, d), lambda i, j: (i, 0)),
            pl.BlockSpec((1, d), row),
            pl.BlockSpec((1, d), row),
            pl.BlockSpec((d, tn), lambda i, j: (0, j)),
        ],
        out_specs=pl.BlockSpec((tm, tn), lambda i, j: (i, j)),
        out_shape=jax.ShapeDtypeStruct((t, n), F32),
        scratch_shapes=[pltpu.VMEM((tm, d), BF16)],
        compiler_params=_cparams(("arbitrary", "arbitrary")),
        name="norm_proj",
    )(x, shift, scale, w)


def _bmm_kernel(a_ref, w_ref, o_ref):
    o_ref[0] = _dot(a_ref[0], w_ref[0])


def _bmm3(a, w):
    nj, k, n = w.shape
    t = a.shape[1]
    tm, tn = 1024, 512
    return pl.pallas_call(
        _bmm_kernel,
        grid=(nj, t // tm, n // tn),
        in_specs=[
            pl.BlockSpec((1, tm, k), lambda j, i, c: (j, i, 0)),
            pl.BlockSpec((1, k, tn), lambda j, i, c: (j, 0, c)),
        ],
        out_specs=pl.BlockSpec((1, tm, tn), lambda j, i, c: (j, i, c)),
        out_shape=jax.ShapeDtypeStruct((nj, t, n), F32),
        compiler_params=_cparams(("arbitrary", "arbitrary", "arbitrary")),
        name="rkv_proj",
    )(a, w)


def _lora_kernel(act, a_ref, w1_ref, w2_ref, o_ref):
    z = _dot(a_ref[0], w1_ref[...])
    if act == "tanh":
        z = jnp.tanh(z)
    elif act == "sigmoid":
        z = _sigmoid(z)
    o_ref[...] = _dot(z.astype(BF16), w2_ref[...])


def _lora(mixes, slab, w1, w2, act):
    _, t, k = mixes.shape
    r = w1.shape[1]
    rp = -(-r // LANES) * LANES
    w1p = jnp.pad(w1, ((0, 0), (0, rp - r)))
    w2p = jnp.pad(w2, ((0, rp - r), (0, 0)))
    n = w2.shape[1]
    tm = 512
    return pl.pallas_call(
        functools.partial(_lora_kernel, act),
        grid=(t // tm,),
        in_specs=[
            pl.BlockSpec((1, tm, k), lambda i: (slab, i, 0)),
            pl.BlockSpec((k, rp), lambda i: (0, 0)),
            pl.BlockSpec((rp, n), lambda i: (0, 0)),
        ],
        out_specs=pl.BlockSpec((tm, n), lambda i: (i, 0)),
        out_shape=jax.ShapeDtypeStruct((t, n), F32),
        compiler_params=_cparams(("arbitrary",)),
        name="lora_" + act,
    )(mixes, w1p, w2p)


def _rwkv_prep_kernel(x_ref, sh_ref, sc_ref, mu_ref, o_ref, carry):
    i = pl.program_id(0)
    tm = x_ref.shape[0]

    @pl.when(i == 0)
    def _():
        carry[...] = jnp.zeros_like(carry)

    h = _modulate(x_ref[...], sh_ref[...], sc_ref[...])
    prev_last = carry[7:8, :]
    rolled = pltpu.roll(h, 1, axis=0)
    rows = lax.broadcasted_iota(I32, h.shape, 0)
    h_prev = jnp.where(rows == 0, prev_last, rolled)
    carry[...] = h[tm - 8:tm, :]
    dx = h_prev - h
    for j in range(6):
        o_ref[j] = (h + dx * mu_ref[j:j + 1, :]).astype(BF16)


def _rwkv_prep(x, shift, scale, mu):
    t, d = x.shape
    tm = 256
    return pl.pallas_call(
        _rwkv_prep_kernel,
        grid=(t // tm,),
        in_specs=[
            pl.BlockSpec((tm, d), lambda i: (i, 0)),
            pl.BlockSpec((1, d), lambda i: (0, 0)),
            pl.BlockSpec((1, d), lambda i: (0, 0)),
            pl.BlockSpec((6, d), lambda i: (0, 0)),
        ],
        out_specs=pl.BlockSpec((6, tm, d), lambda i: (0, i, 0)),
        out_shape=jax.ShapeDtypeStruct((6, t, d), BF16),
        scratch_shapes=[pltpu.VMEM((8, d), F32)],
        compiler_params=_cparams(("arbitrary",)),
        name="rwkv_prep",
    )(x, shift, scale, mu)


def _split2(x):
    hi = x.astype(BF16)
    lo = (x - hi.astype(F32)).astype(BF16)
    return hi, lo


def _bdot(a, b):
    return lax.dot_general(a, b, (((2,), (1,)), ((0,), (0,))), preferred_element_type=F32)


def _bdot_nt(a, b):
    return lax.dot_general(a, b, (((2,), (2,)), ((0,), (0,))), preferred_element_type=F32)


def _bdot_tn(a, b):
    return lax.dot_general(a, b, (((1,), (1,)), ((0,), (0,))), preferred_element_type=F32)


def _rwkv_chunk_kernel(has_vres, n_chunks, *refs):
    if has_vres:
        (r_ref, k_ref, v_ref, lw_ref, la_ref, lg_ref, lv_ref, vf_ref,
         w0_ref, a0_ref, kk_ref, ka_ref, rk_ref, lnw_ref, lnb_ref, v0_ref,
         o_ref, s_scr) = refs
    else:
        (r_ref, k_ref, v_ref, lw_ref, la_ref, lg_ref,
         w0_ref, a0_ref, kk_ref, ka_ref, rk_ref, lnw_ref, lnb_ref,
         o_ref, s_scr) = refs

    L = CHUNK
    W = LANES
    G = n_chunks
    R = G * L

    @pl.when(pl.program_id(1) == 0)
    def _():
        s_scr[...] = jnp.zeros_like(s_scr)

    lane = lax.broadcasted_iota(I32, (1, W), 1)
    hm0 = (lane < A_HEAD).astype(F32)
    hm1 = 1.0 - hm0
    rr = lax.broadcasted_iota(I32, (L, L), 0)
    cc = lax.broadcasted_iota(I32, (L, L), 1)
    low_strict = (rr > cc)[None]
    low_incl = (rr >= cc)[None]
    eye_l = (rr == cc).astype(F32)[None]
    diag_blk = ((rr >> 3) == (cc >> 3))[None]
    merge_blks = [(((rr >> (s + 1)) == (cc >> (s + 1))) & ((rr >> s) > (cc >> s)))[None]
                  for s in (3, 4, 5)]
    r2 = lax.broadcasted_iota(I32, (W, W), 0)
    c2 = lax.broadcasted_iota(I32, (W, W), 1)
    same_head = (r2 // A_HEAD) == (c2 // A_HEAD)
    seg_ones = same_head.astype(BF16)
    r3 = lax.broadcasted_iota(I32, (R, R), 0)
    c3 = lax.broadcasted_iota(I32, (R, R), 1)
    tri_ones = (((r3 // L) == (c3 // L)) & (r3 >= c3)).astype(BF16)

    def segsum(x):
        hi, lo = _split2(x)
        return _dot(hi, seg_ones) + _dot(lo, seg_ones)

    r = r_ref[0]
    k = k_ref[0]
    v = v_ref[0]
    z = -(w0_ref[...] + lw_ref[...])
    softplus = jnp.maximum(z, 0.0) + jnp.log(1.0 + jnp.exp(-jnp.abs(z)))
    logw = -jnp.exp(-softplus - 0.5)
    a = _sigmoid(a0_ref[...] + la_ref[...])
    if has_vres:
        v = v + (vf_ref[...] - v) * _sigmoid(v0_ref[...] + lv_ref[...])
    kk = k * kk_ref[...]
    kk = kk / jnp.maximum(jnp.sqrt(segsum(kk * kk)), 1e-12)
    k = k * (1.0 + (a - 1.0) * ka_ref[...])

    h1 = logw.astype(BF16)
    rem = logw - h1.astype(F32)
    h2 = rem.astype(BF16)
    h3 = (rem - h2.astype(F32)).astype(BF16)
    cum = _dot(tri_ones, h1) + _dot(tri_ones, h2) + _dot(tri_ones, h3)
    w_in = jnp.exp(cum)
    w_inv = jnp.exp(-cum)
    a_t = -kk * jnp.exp(cum - logw)
    b_t = kk * a * w_inv
    k_t = k * w_inv
    r_t = r * w_in

    def c3d(x):
        return x.reshape(G, L, W)

    def both(x, masked):
        x3 = c3d(x)
        if masked:
            return jnp.concatenate([x3 * hm0, x3 * hm1], axis=0).astype(BF16)
        x3 = x3.astype(BF16)
        return jnp.concatenate([x3, x3], axis=0)

    a_e = both(a_t, True)
    r_e = both(r_t, True)
    v_e = both(v, True)
    b_2 = both(b_t, False)
    k_2 = both(k_t, False)

    x_ab = jnp.where(low_strict, _bdot_nt(a_e, b_2), 0.0)
    x_ak = jnp.where(low_strict, _bdot_nt(a_e, k_2), 0.0).astype(BF16)
    x_rb = jnp.where(low_incl, _bdot_nt(r_e, b_2), 0.0).astype(BF16)
    x_rk = jnp.where(low_incl, _bdot_nt(r_e, k_2), 0.0).astype(BF16)

    xp = jnp.where(diag_blk, x_ab, 0.0)
    tm = eye_l + xp
    for _ in range(2):
        xb = xp.astype(BF16)
        xp = _bdot(xb, xb)
        tm = tm + _bdot(tm.astype(BF16), xp.astype(BF16))
    for off_blk in merge_blks:
        tb = tm.astype(BF16)
        x_off = jnp.where(off_blk, x_ab, 0.0).astype(BF16)
        tm = tm + _bdot(_bdot(tb, x_off).astype(BF16), tb)
    tb = tm.astype(BF16)

    p_e = _bdot(tb, a_e)
    q_e = _bdot(_bdot(tb, x_ak).astype(BF16), v_e)
    p_eb = p_e.astype(BF16)
    q_eb = q_e.astype(BF16)
    rp_e = _bdot(x_rb, p_eb)
    y0_e = _bdot(x_rb, q_eb) + _bdot(x_rk, v_e)
    p_sum = p_e[:G] + p_e[G:]
    q_sum = q_e[:G] + q_e[G:]
    rp = (c3d(r_t) + rp_e[:G] + rp_e[G:]).astype(BF16)
    y0 = y0_e[:G] + y0_e[G:]

    b_3 = b_2[:G]
    k_3 = k_2[:G]
    w_last = c3d(w_in)[:, L - 1:L, :]
    m_lr = (jnp.where(same_head[None], _bdot_tn(p_sum.astype(BF16), b_3), 0.0) * w_last).astype(BF16)
    qv = jnp.concatenate([q_sum, c3d(v)], axis=1).astype(BF16)
    bk = jnp.concatenate([b_3, k_3], axis=1)
    c_bd = jnp.where(same_head[None], _bdot_tn(qv, bk), 0.0) * w_last

    s = s_scr[...]
    ys = []
    for c in range(G):
        s_hi, s_lo = _split2(s)
        ys.append(_dot_nt(rp[c], s_hi) + _dot_nt(rp[c], s_lo) + y0[c])
        s = s * w_last[c] + _dot(s_hi, m_lr[c]) + _dot(s_lo, m_lr[c]) + c_bd[c]
    s_scr[...] = s
    y = jnp.concatenate(ys, axis=0)

    mean = segsum(y) * (1.0 / A_HEAD)
    dlt = y - mean
    var = segsum(dlt * dlt) * (1.0 / A_HEAD)
    yn = dlt * lax.rsqrt(var + A_GN_EPS) * lnw_ref[...] + lnb_ref[...]
    bonus = segsum(r * k * rk_ref[...]) * v
    o_ref[...] = ((yn + bonus) * lg_ref[...]).astype(BF16)


def _rwkv_chunks(rkv, lw, la, lg, lv, v_first, w0, a0, k_k, k_a, r_k, ln_w, ln_b, v0):
    _, t, d = rkv.shape
    n_chunks = 8
    tm = n_chunks * CHUNK
    has_vres = lv is not None
    blk = lambda p, c: (c, p)
    prm = lambda p, c: (0, p)
    slab = lambda j: pl.BlockSpec((1, tm, LANES), lambda p, c: (j, c, p))
    in_specs = [slab(0), slab(1), slab(2)] + [pl.BlockSpec((tm, LANES), blk)] * (5 if has_vres else 3)
    in_specs += [pl.BlockSpec((1, LANES), prm)] * (8 if has_vres else 7)
    args = [rkv, rkv, rkv, lw, la, lg]
    if has_vres:
        args += [lv, v_first]
    args += [w0, a0, k_k, k_a, r_k, ln_w, ln_b]
    if has_vres:
        args += [v0]
    return pl.pallas_call(
        functools.partial(_rwkv_chunk_kernel, has_vres, n_chunks),
        grid=(d // LANES, t // tm),
        in_specs=in_specs,
        out_specs=pl.BlockSpec((tm, LANES), blk),
        out_shape=jax.ShapeDtypeStruct((t, d), BF16),
        scratch_shapes=[pltpu.VMEM((LANES, LANES), F32)],
        compiler_params=_cparams(("arbitrary", "arbitrary")),
        name="rwkv_chunks",
    )(*args)


def _dsa_split_kernel(p_ref, kvn_ref, kin_ref, q_ref, qi_ref, ckv_ref, kidx_ref, wi_ref):
    tm = p_ref.shape[0]
    nb = tm // Q_BLOCK
    o = 0
    for b in range(nb):
        rows = slice(b * Q_BLOCK, (b + 1) * Q_BLOCK)
        for h in range(B_HEADS):
            q_ref[b, h] = p_ref[rows, o + h * B_HEAD_DIM:o + (h + 1) * B_HEAD_DIM].astype(BF16)
    o += B_Q_W
    ckv = p_ref[:, o:o + B_KV_RANK]
    ckv = ckv * lax.rsqrt(jnp.mean(ckv * ckv, axis=-1, keepdims=True) + RMS_EPS) * kvn_ref[...]
    ckv_ref[...] = ckv.astype(BF16)
    o += B_KV_RANK
    for b in range(nb):
        rows = slice(b * Q_BLOCK, (b + 1) * Q_BLOCK)
        for h in range(B_IDX_HEADS):
            qi_ref[b, h] = p_ref[rows, o + h * B_IDX_DIM:o + (h + 1) * B_IDX_DIM].astype(BF16)
    o += B_QI_W
    kidx = p_ref[:, o:o + B_IDX_DIM]
    kidx = kidx * lax.rsqrt(jnp.mean(kidx * kidx, axis=-1, keepdims=True) + RMS_EPS) * kin_ref[...]
    kidx_ref[...] = kidx.astype(BF16)
    o += B_IDX_DIM
    wi_ref[...] = p_ref[:, o:o + LANES] * B_IDX_W_SCALE


def _dsa_split(proj, kv_norm, kidx_norm):
    t, n = proj.shape
    tm = 256
    nb = tm // Q_BLOCK
    return pl.pallas_call(
        _dsa_split_kernel,
        grid=(t // tm,),
        in_specs=[
            pl.BlockSpec((tm, n), lambda i: (i, 0)),
            pl.BlockSpec((1, B_KV_RANK), lambda i: (0, 0)),
            pl.BlockSpec((1, B_IDX_DIM), lambda i: (0, 0)),
        ],
        out_specs=[
            pl.BlockSpec((nb, B_HEADS, Q_BLOCK, B_HEAD_DIM), lambda i: (i, 0, 0, 0)),
            pl.BlockSpec((nb, B_IDX_HEADS, Q_BLOCK, B_IDX_DIM), lambda i: (i, 0, 0, 0)),
            pl.BlockSpec((tm, B_KV_RANK), lambda i: (i, 0)),
            pl.BlockSpec((tm, B_IDX_DIM), lambda i: (i, 0)),
            pl.BlockSpec((tm, LANES), lambda i: (i, 0)),
        ],
        out_shape=[
            jax.ShapeDtypeStruct((t // Q_BLOCK, B_HEADS, Q_BLOCK, B_HEAD_DIM), BF16),
            jax.ShapeDtypeStruct((t // Q_BLOCK, B_IDX_HEADS, Q_BLOCK, B_IDX_DIM), BF16),
            jax.ShapeDtypeStruct((t, B_KV_RANK), BF16),
            jax.ShapeDtypeStruct((t, B_IDX_DIM), BF16),
            jax.ShapeDtypeStruct((t, LANES), F32),
        ],
        compiler_params=_cparams(("arbitrary",)),
        name="dsa_split",
    )(proj, kv_norm, kidx_norm)


def _dsa_attn_kernel(topk, tk, idx_bits,
                     q_ref, qi_ref, wrow_ref, kidx_ref, ckv_ref, ckvt_ref, wuk_ref, wuv_ref, o_ref,
                     key_scr, qlat_scr, acc_scr, m_scr, l_scr, j0_scr):
    i = pl.program_id(0)
    QB = Q_BLOCK
    n_kt = (i * QB + QB + tk - 1) // tk
    qpos = i * QB + lax.broadcasted_iota(I32, (1, QB), 1)
    limit = (qpos // CHUNK + 1) * CHUNK

    def key_pos(off):
        return off + lax.broadcasted_iota(I32, (tk, QB), 0)

    for h in range(B_HEADS):
        ql = _dot(q_ref[0, h], wuk_ref[h]) * B_SCALE
        qlat_scr[h * QB:(h + 1) * QB, :] = ql.astype(BF16)

    qi = qi_ref[0].reshape(B_IDX_HEADS * QB, B_IDX_DIM)
    wrow = wrow_ref[0]

    def score_body(kt, carry):
        off = pl.multiple_of(kt * tk, tk)
        rel = jnp.maximum(_dot_nt(kidx_ref[pl.ds(off, tk), :], qi), 0.0) * wrow
        sc = rel[:, 0:QB]
        for h in range(1, B_IDX_HEADS):
            sc = sc + rel[:, h * QB:(h + 1) * QB]
        sc = jnp.where(sc == 0.0, 0.0, sc)
        bits = pltpu.bitcast(sc, I32)
        key = bits ^ ((bits >> 31) & INT_MAX)
        key_scr[pl.ds(off, tk), :] = jnp.where(key_pos(off) < limit, key, INT_MIN)
        return carry

    lax.fori_loop(0, n_kt, score_body, 0)

    def count_keys(pred):
        def body(kt, acc):
            off = pl.multiple_of(kt * tk, tk)
            hit = jnp.where(pred(key_scr[pl.ds(off, tk), :], off), 1.0, 0.0)
            return acc + jnp.sum(hit.reshape(tk // 8, 8, QB), axis=0)
        acc = lax.fori_loop(0, n_kt, body, jnp.zeros((8, QB), F32))
        return jnp.sum(acc, axis=0, keepdims=True)

    def bit_body(b, thr_u):
        cand = thr_u | jnp.left_shift(jnp.int32(1), 31 - b)
        cand_s = cand ^ INT_MIN
        cnt = count_keys(lambda key, off: key >= cand_s)
        return jnp.where(cnt >= topk, cand, thr_u)

    thr = lax.fori_loop(0, 32, bit_body, jnp.zeros((1, QB), I32)) ^ INT_MIN
    cnt_gt = count_keys(lambda key, off: key > thr)
    cnt_ge = count_keys(lambda key, off: key >= thr)
    need = topk - cnt_gt

    j0_scr[...] = jnp.full((1, QB), INT_MAX, I32)
    excess = jnp.max(jnp.where(thr > INT_MIN, cnt_ge, 0.0)) > topk

    @pl.when(excess)
    def _():
        def tie_body(b, j0):
            cand = j0 | jnp.left_shift(jnp.int32(1), idx_bits - 1 - b)
            cnt = count_keys(lambda key, off: (key == thr) & (key_pos(off) < cand))
            return jnp.where(cnt < need, cand, j0)
        j0_scr[...] = lax.fori_loop(0, idx_bits, tie_body, jnp.zeros((1, QB), I32))

    j0 = j0_scr[...]

    m_scr[...] = jnp.full(m_scr.shape, NEG_BIG, F32)
    l_scr[...] = jnp.zeros_like(l_scr)
    acc_scr[...] = jnp.zeros_like(acc_scr)

    def att_body(kt, carry):
        off = pl.multiple_of(kt * tk, tk)
        key = key_scr[pl.ds(off, tk), :]
        sel = ((key > thr) | ((key == thr) & (key_pos(off) <= j0))) & (key > INT_MIN)
        bias = jnp.where(sel, 0.0, NEG_BIG)
        s = _dot_nt(ckv_ref[pl.ds(off, tk), :], qlat_scr[...])
        s = s + jnp.concatenate([bias] * B_HEADS, axis=1)
        m_old = m_scr[...]
        m_new = jnp.maximum(m_old, jnp.max(s, axis=0, keepdims=True))
        alpha = jnp.exp(m_old - m_new)
        p = jnp.exp(s - m_new)
        l_scr[...] = alpha * l_scr[...] + jnp.sum(p, axis=0, keepdims=True)
        acc_scr[...] = alpha * acc_scr[...] + _dot(ckvt_ref[:, pl.ds(off, tk)], p.astype(BF16))
        m_scr[...] = m_new
        return carry

    lax.fori_loop(0, n_kt, att_body, 0)

    o_lat_t = (acc_scr[...] / l_scr[...]).astype(BF16)
    for h in range(B_HEADS):
        oh = _dot_tn(o_lat_t[:, h * QB:(h + 1) * QB], wuv_ref[h])
        o_ref[:, h * B_HEAD_DIM:(h + 1) * B_HEAD_DIM] = oh.astype(BF16)


def _dsa_attn(q, qi, wrow, kidx, ckv, ckvt, w_uk, w_uv):
    nb = q.shape[0]
    t = kidx.shape[0]
    tk = 512
    topk = min(TOPK_MAX, t // 4)
    idx_bits = max(1, (t - 1).bit_length())
    const2 = lambda i: (0, 0)
    const3 = lambda i: (0, 0, 0)
    hq = B_HEADS * Q_BLOCK
    once = pl.Buffered(1)
    return pl.pallas_call(
        functools.partial(_dsa_attn_kernel, topk, tk, idx_bits),
        grid=(nb,),
        in_specs=[
            pl.BlockSpec((1, B_HEADS, Q_BLOCK, B_HEAD_DIM), lambda i: (i, 0, 0, 0)),
            pl.BlockSpec((1, B_IDX_HEADS, Q_BLOCK, B_IDX_DIM), lambda i: (i, 0, 0, 0)),
            pl.BlockSpec((1, 1, B_IDX_HEADS * Q_BLOCK), lambda i: (i, 0, 0)),
            pl.BlockSpec((t, B_IDX_DIM), const2, pipeline_mode=once),
            pl.BlockSpec((t, B_KV_RANK), const2, pipeline_mode=once),
            pl.BlockSpec((B_KV_RANK, t), const2, pipeline_mode=once),
            pl.BlockSpec((B_HEADS, B_HEAD_DIM, B_KV_RANK), const3, pipeline_mode=once),
            pl.BlockSpec((B_HEADS, B_KV_RANK, B_HEAD_DIM), const3, pipeline_mode=once),
        ],
        out_specs=pl.BlockSpec((Q_BLOCK, B_Q_W), lambda i: (i, 0)),
        out_shape=jax.ShapeDtypeStruct((t, B_Q_W), BF16),
        scratch_shapes=[
            pltpu.VMEM((t, Q_BLOCK), I32),
            pltpu.VMEM((hq, B_KV_RANK), BF16),
            pltpu.VMEM((B_KV_RANK, hq), F32),
            pltpu.VMEM((1, hq), F32),
            pltpu.VMEM((1, hq), F32),
            pltpu.VMEM((1, Q_BLOCK), I32),
        ],
        compiler_params=_cparams(("arbitrary",)),
        name="dsa_attn",
    )(q, qi, wrow, kidx, ckv, ckvt, w_uk, w_uv)


def _final_norm_kernel(x_ref, w_ref, o_ref):
    x = x_ref[...]
    o_ref[...] = x * lax.rsqrt(jnp.mean(x * x, axis=-1, keepdims=True) + RMS_EPS) * w_ref[...]


def _final_norm(x, w):
    t, d = x.shape
    tm = 512
    return pl.pallas_call(
        _final_norm_kernel,
        grid=(t // tm,),
        in_specs=[pl.BlockSpec((tm, d), lambda i: (i, 0)), pl.BlockSpec((1, d), lambda i: (0, 0))],
        out_specs=pl.BlockSpec((tm, d), lambda i: (i, 0)),
        out_shape=jax.ShapeDtypeStruct((t, d), F32),
        compiler_params=_cparams(("arbitrary",)),
        name="final_norm",
    )(x, w)


def _row(v):
    return v.reshape(1, -1)


def _rwkv_layer(x, shift, scale, gate, v_first, mu, w_rkv, w0, w1, w2, a0, a1, a2,
                g1, g2, k_k, k_a, r_k, ln_w, ln_b, w_o, vres):
    mixes = _rwkv_prep(x, shift, scale, mu[jnp.array([0, 2, 3, 1, 4, 5])])
    rkv = _bmm3(mixes, w_rkv.astype(BF16))
    lw = _lora(mixes, 3, w1.astype(BF16), w2.astype(BF16), "tanh")
    la = _lora(mixes, 4, a1.astype(BF16), a2.astype(BF16), "none")
    lg = _lora(mixes, 5, g1.astype(BF16), g2.astype(BF16), "sigmoid")
    if vres is None:
        lv, v0 = None, None
        v_first = rkv[2]
    else:
        v0, v1, v2 = vres
        lv = _lora(mixes, 2, v1.astype(BF16), v2.astype(BF16), "none")
        v0 = _row(v0)
    y = _rwkv_chunks(rkv, lw, la, lg, lv, v_first, _row(w0), _row(a0), _row(k_k), _row(k_a),
                     _row(r_k), _row(ln_w), _row(ln_b), v0)
    return _proj_res(y, w_o.astype(BF16), x, gate), v_first


def _dsa_layer(x, shift, scale, gate, w_in, kv_norm, kidx_norm, w_uk, w_uv, w_o):
    t = x.shape[0]
    n_in = w_in.shape[1]
    n_pad = -(-(n_in - B_IDX_HEADS + LANES) // 768) * 768
    w_in_p = jnp.pad(w_in, ((0, 0), (0, n_pad - n_in))).astype(BF16)
    proj = _norm_proj(x, shift, scale, w_in_p, 768)
    q, qi, ckv, kidx, wi = _dsa_split(proj, _row(kv_norm), _row(kidx_norm))
    nb = t // Q_BLOCK
    wrow = wi[:, :B_IDX_HEADS].reshape(nb, Q_BLOCK, B_IDX_HEADS).transpose(0, 2, 1)
    wrow = wrow.reshape(nb, 1, B_IDX_HEADS * Q_BLOCK)
    o = _dsa_attn(q, qi, wrow, kidx, ckv, ckv.T, w_uk.astype(BF16), w_uv.astype(BF16))
    return _proj_res(o, w_o.astype(BF16), x, gate)


def kernel(x, c, ada_w, ada_b, a_mu, a_w_rkv, a_w0, a_w1, a_w2, a_a0, a_a1, a_a2, a_v0, a_v1, a_v2, a_g1, a_g2, a_k_k, a_k_a, a_r_k, a_ln_w, a_ln_b, a_w_o, b_w_in, b_kv_norm, b_kidx_norm, b_w_uk, b_w_uv, b_w_o, f_w1, f_w3, f_w2, final_norm):
    b, t, d = x.shape
    assert b == 1 and d == D_MODEL
    mod_all = _ada(c, ada_w, ada_b)
    xs = x.reshape(t, d)
    v_first = None
    for i in range(DEPTH):
        shift1, scale1, gate1, shift2, scale2, gate2 = (
            mod_all[i, :, s * d:(s + 1) * d] for s in range(6))
        j = i // 2
        if i % 2 == 0:
            vres = None if j == 0 else (a_v0[j - 1], a_v1[j - 1], a_v2[j - 1])
            xs, v_first = _rwkv_layer(
                xs, shift1, scale1, gate1, v_first, a_mu[j], a_w_rkv[j], a_w0[j], a_w1[j], a_w2[j],
                a_a0[j], a_a1[j], a_a2[j], a_g1[j], a_g2[j], a_k_k[j], a_k_a[j], a_r_k[j],
                a_ln_w[j], a_ln_b[j], a_w_o[j], vres)
        else:
            xs = _dsa_layer(xs, shift1, scale1, gate1, b_w_in[j], b_kv_norm[j], b_kidx_norm[j],
                            b_w_uk[j], b_w_uv[j], b_w_o[j])
        xs = _ffn(xs, shift2, scale2, gate2, f_w1[i].astype(BF16), f_w3[i].astype(BF16),
                  f_w2[i].astype(BF16))
    return _final_norm(xs, _row(final_norm)).reshape(b, t, d)
```

```python
import functools

import jax
import jax.numpy as jnp
from jax import lax
from jax.experimental import pallas as pl
from jax.experimental.pallas import tpu as pltpu

F32 = jnp.float32
BF16 = jnp.bfloat16
I32 = jnp.int32

D_MODEL = 2048
DEPTH = 4
CHUNK = 64
RMS_EPS = 1e-6
A_HEAD = 64
A_HEADS = D_MODEL // A_HEAD
A_GN_EPS = A_HEAD * 1e-5
B_HEADS = 16
B_HEAD_DIM = 128
B_KV_RANK = 512
B_IDX_HEADS = 8
B_IDX_DIM = 128
TOPK_MAX = 256
Q_BLOCK = 128
B_Q_W = B_HEADS * B_HEAD_DIM
B_QI_W = B_IDX_HEADS * B_IDX_DIM
B_SCALE = B_HEAD_DIM ** -0.5
B_IDX_W_SCALE = (B_IDX_HEADS ** -0.5) * (B_IDX_DIM ** -0.5)
D_FF = 5632

LANES = 128
VMEM_LIMIT = 56 * 1024 * 1024
INT_MIN = -(2 ** 31)
INT_MAX = 2 ** 31 - 1
NEG_BIG = -1e30
LOG2_E = 1.4426950408889634
COUNT_WAYS = 8
ATT_HEADS_PER_STEP = 4
RWKV_LANES = 256

_NT = (((1,), (1,)), ((), ()))
_TN = (((0,), (0,)), ((), ()))


def _cparams(sem):
    return pltpu.CompilerParams(dimension_semantics=sem, vmem_limit_bytes=VMEM_LIMIT)


def _dot(a, b):
    return jnp.dot(a, b, preferred_element_type=F32)


def _dot_nt(a, b):
    return lax.dot_general(a, b, _NT, preferred_element_type=F32)


def _dot_tn(a, b):
    return lax.dot_general(a, b, _TN, preferred_element_type=F32)


def _modulate(x, shift, scale):
    ms = jnp.mean(x * x, axis=-1, keepdims=True)
    return x * lax.rsqrt(ms + RMS_EPS) * (1.0 + scale) + shift


def _sigmoid(x):
    return 1.0 / (1.0 + jnp.exp(-x))


def _ada_kernel(c_ref, w_ref, b_ref, o_ref):
    c = c_ref[...]
    s = c * _sigmoid(c)
    o_ref[0] = jnp.sum(s * w_ref[0], axis=0, keepdims=True) + b_ref[0]


def _ada(c, ada_w, ada_b):
    depth, d, n = ada_w.shape
    tn = 1024
    return pl.pallas_call(
        _ada_kernel,
        grid=(depth, n // tn),
        in_specs=[
            pl.BlockSpec((d, 1), lambda l, j: (0, 0)),
            pl.BlockSpec((1, d, tn), lambda l, j: (l, 0, j)),
            pl.BlockSpec((1, 1, tn), lambda l, j: (l, 0, j)),
        ],
        out_specs=pl.BlockSpec((1, 1, tn), lambda l, j: (l, 0, j)),
        out_shape=jax.ShapeDtypeStruct((depth, 1, n), F32),
        compiler_params=_cparams(("arbitrary", "arbitrary")),
        name="ada_mod",
    )(c.reshape(d, 1), ada_w, ada_b.reshape(depth, 1, n))


def _ffn_kernel(x_ref, sh_ref, sc_ref, g_ref, w1_ref, w3_ref, w2_ref, o_ref, h_scr, acc_scr):
    f = pl.program_id(1)

    @pl.when(f == 0)
    def _():
        h_scr[...] = _modulate(x_ref[...], sh_ref[...], sc_ref[...]).astype(BF16)
        acc_scr[...] = jnp.zeros_like(acc_scr)

    h = h_scr[...]
    a = _dot(h, w1_ref[...])
    b = _dot(h, w3_ref[...])
    u = (a * _sigmoid(a) * b).astype(BF16)
    acc_scr[...] += _dot(u, w2_ref[...])

    @pl.when(f == pl.num_programs(1) - 1)
    def _():
        o_ref[...] = x_ref[...] + g_ref[...] * acc_scr[...]


def _ffn(x, shift, scale, gate, w1, w3, w2):
    t, d = x.shape
    dff = w1.shape[1]
    tm, tf = 512, 512
    row = lambda i, f: (0, 0)
    return pl.pallas_call(
        _ffn_kernel,
        grid=(t // tm, dff // tf),
        in_specs=[
            pl.BlockSpec((tm, d), lambda i, f: (i, 0)),
            pl.BlockSpec((1, d), row),
            pl.BlockSpec((1, d), row),
            pl.BlockSpec((1, d), row),
            pl.BlockSpec((d, tf), lambda i, f: (0, f)),
            pl.BlockSpec((d, tf), lambda i, f: (0, f)),
            pl.BlockSpec((tf, d), lambda i, f: (f, 0)),
        ],
        out_specs=pl.BlockSpec((tm, d), lambda i, f: (i, 0)),
        out_shape=jax.ShapeDtypeStruct((t, d), F32),
        scratch_shapes=[pltpu.VMEM((tm, d), BF16), pltpu.VMEM((tm, d), F32)],
        compiler_params=_cparams(("arbitrary", "arbitrary")),
        name="ffn",
    )(x, shift, scale, gate, w1, w3, w2)


def _proj_res_kernel(a_ref, w_ref, x_ref, g_ref, o_ref):
    o_ref[...] = x_ref[...] + g_ref[...] * _dot(a_ref[...], w_ref[...])


def _proj_res(a, w, x, gate):
    t, k = a.shape
    n = w.shape[1]
    tm, tn = 1024, 512
    return pl.pallas_call(
        _proj_res_kernel,
        grid=(t // tm, n // tn),
        in_specs=[
            pl.BlockSpec((tm, k), lambda i, j: (i, 0)),
            pl.BlockSpec((k, tn), lambda i, j: (0, j)),
            pl.BlockSpec((tm, tn), lambda i, j: (i, j)),
            pl.BlockSpec((1, tn), lambda i, j: (0, j)),
        ],
        out_specs=pl.BlockSpec((tm, tn), lambda i, j: (i, j)),
        out_shape=jax.ShapeDtypeStruct((t, n), F32),
        compiler_params=_cparams(("arbitrary", "arbitrary")),
        name="proj_res",
    )(a, w, x, gate)


def _norm_proj_kernel(x_ref, sh_ref, sc_ref, w_ref, o_ref, h_scr):
    @pl.when(pl.program_id(1) == 0)
    def _():
        h_scr[...] = _modulate(x_ref[...], sh_ref[...], sc_ref[...]).astype(BF16)

    o_ref[...] = _dot(h_scr[...], w_ref[...])


def _norm_proj(x, shift, scale, w, tn):
    t, d = x.shape
    n = w.shape[1]
    tm = 512
    row = lambda i, j: (0, 0)
    return pl.pallas_call(
        _norm_proj_kernel,
        grid=(t // tm, n // tn),
        in_specs=[
            pl.BlockSpec((tm, d), lambda i, j: (i, 0)),
            pl.BlockSpec((1, d), row),
            pl.BlockSpec((1, d), row),
            pl.BlockSpec((d, tn), lambda i, j: (0, j)),
        ],
        out_specs=pl.BlockSpec((tm, tn), lambda i, j: (i, j)),
        out_shape=jax.ShapeDtypeStruct((t, n), F32),
        scratch_shapes=[pltpu.VMEM((tm, d), BF16)],
        compiler_params=_cparams(("arbitrary", "arbitrary")),
        name="norm_proj",
    )(x, shift, scale, w)


def _bmm_kernel(a_ref, w_ref, o_ref):
    o_ref[0] = _dot(a_ref[0], w_ref[0])


def _bmm3(a, w):
    nj, k, n = w.shape
    t = a.shape[1]
    tm, tn = 1024, 512
    return pl.pallas_call(
        _bmm_kernel,
        grid=(nj, t // tm, n // tn),
        in_specs=[
            pl.BlockSpec((1, tm, k), lambda j, i, c: (j, i, 0)),
            pl.BlockSpec((1, k, tn), lambda j, i, c: (j, 0, c)),
        ],
        out_specs=pl.BlockSpec((1, tm, tn), lambda j, i, c: (j, i, c)),
        out_shape=jax.ShapeDtypeStruct((nj, t, n), F32),
        compiler_params=_cparams(("arbitrary", "arbitrary", "arbitrary")),
        name="rkv_proj",
    )(a, w)


def _lora_kernel(act, a_ref, w1_ref, w2_ref, o_ref):
    z = _dot(a_ref[0], w1_ref[...])
    if act == "tanh":
        z = jnp.tanh(z)
    elif act == "sigmoid":
        z = _sigmoid(z)
    o_ref[...] = _dot(z.astype(BF16), w2_ref[...])


def _lora(mixes, slab, w1, w2, act):
    _, t, k = mixes.shape
    r = w1.shape[1]
    rp = -(-r // LANES) * LANES
    w1p = jnp.pad(w1, ((0, 0), (0, rp - r)))
    w2p = jnp.pad(w2, ((0, rp - r), (0, 0)))
    n = w2.shape[1]
    tm = 512
    return pl.pallas_call(
        functools.partial(_lora_kernel, act),
        grid=(t // tm,),
        in_specs=[
            pl.BlockSpec((1, tm, k), lambda i: (slab, i, 0)),
            pl.BlockSpec((k, rp), lambda i: (0, 0)),
            pl.BlockSpec((rp, n), lambda i: (0, 0)),
        ],
        out_specs=pl.BlockSpec((tm, n), lambda i: (i, 0)),
        out_shape=jax.ShapeDtypeStruct((t, n), F32),
        compiler_params=_cparams(("arbitrary",)),
        name="lora_" + act,
    )(mixes, w1p, w2p)


def _rwkv_prep_kernel(x_ref, sh_ref, sc_ref, mu_ref, o_ref, carry):
    i = pl.program_id(0)
    tm = x_ref.shape[0]

    @pl.when(i == 0)
    def _():
        carry[...] = jnp.zeros_like(carry)

    h = _modulate(x_ref[...], sh_ref[...], sc_ref[...])
    prev_last = carry[7:8, :]
    rolled = pltpu.roll(h, 1, axis=0)
    rows = lax.broadcasted_iota(I32, h.shape, 0)
    h_prev = jnp.where(rows == 0, prev_last, rolled)
    carry[...] = h[tm - 8:tm, :]
    dx = h_prev - h
    for j in range(6):
        o_ref[j] = (h + dx * mu_ref[j:j + 1, :]).astype(BF16)


def _rwkv_prep(x, shift, scale, mu):
    t, d = x.shape
    tm = 256
    return pl.pallas_call(
        _rwkv_prep_kernel,
        grid=(t // tm,),
        in_specs=[
            pl.BlockSpec((tm, d), lambda i: (i, 0)),
            pl.BlockSpec((1, d), lambda i: (0, 0)),
            pl.BlockSpec((1, d), lambda i: (0, 0)),
            pl.BlockSpec((6, d), lambda i: (0, 0)),
        ],
        out_specs=pl.BlockSpec((6, tm, d), lambda i: (0, i, 0)),
        out_shape=jax.ShapeDtypeStruct((6, t, d), BF16),
        scratch_shapes=[pltpu.VMEM((8, d), F32)],
        compiler_params=_cparams(("arbitrary",)),
        name="rwkv_prep",
    )(x, shift, scale, mu)


def _split2(x):
    hi = x.astype(BF16)
    lo = (x - hi.astype(F32)).astype(BF16)
    return hi, lo


def _bdot(a, b):
    return lax.dot_general(a, b, (((2,), (1,)), ((0,), (0,))), preferred_element_type=F32)


def _bdot_nt(a, b):
    return lax.dot_general(a, b, (((2,), (2,)), ((0,), (0,))), preferred_element_type=F32)


def _bdot_tn(a, b):
    return lax.dot_general(a, b, (((1,), (1,)), ((0,), (0,))), preferred_element_type=F32)


def _rwkv_chunk_kernel(has_vres, n_chunks, *refs):
    if has_vres:
        (r_ref, k_ref, v_ref, lw_ref, la_ref, lg_ref, lv_ref, vf_ref,
         w0_ref, a0_ref, kk_ref, ka_ref, rk_ref, lnw_ref, lnb_ref, v0_ref,
         o_ref, s_scr) = refs
    else:
        (r_ref, k_ref, v_ref, lw_ref, la_ref, lg_ref,
         w0_ref, a0_ref, kk_ref, ka_ref, rk_ref, lnw_ref, lnb_ref,
         o_ref, s_scr) = refs

    L = CHUNK
    W = o_ref.shape[1]
    NH = W // A_HEAD
    G = n_chunks
    R = G * L

    @pl.when(pl.program_id(1) == 0)
    def _():
        s_scr[...] = jnp.zeros_like(s_scr)

    lane = lax.broadcasted_iota(I32, (1, W), 1)
    head_masks = [((lane // A_HEAD) == e).astype(F32) for e in range(NH)]
    rr = lax.broadcasted_iota(I32, (L, L), 0)
    cc = lax.broadcasted_iota(I32, (L, L), 1)
    low_strict = (rr > cc)[None]
    low_incl = (rr >= cc)[None]
    eye_l = (rr == cc).astype(F32)[None]
    diag_blk = ((rr >> 3) == (cc >> 3))[None]
    merge_blks = [(((rr >> (s + 1)) == (cc >> (s + 1))) & ((rr >> s) > (cc >> s)))[None]
                  for s in (3, 4, 5)]
    r2 = lax.broadcasted_iota(I32, (W, W), 0)
    c2 = lax.broadcasted_iota(I32, (W, W), 1)
    same_head = (r2 // A_HEAD) == (c2 // A_HEAD)
    seg_ones = same_head.astype(BF16)
    r3 = lax.broadcasted_iota(I32, (R, R), 0)
    c3 = lax.broadcasted_iota(I32, (R, R), 1)
    tri_ones = (((r3 // L) == (c3 // L)) & (r3 >= c3)).astype(BF16)

    def segsum(x):
        hi, lo = _split2(x)
        return _dot(hi, seg_ones) + _dot(lo, seg_ones)

    r = r_ref[0]
    k = k_ref[0]
    v = v_ref[0]
    z = -(w0_ref[...] + lw_ref[...])
    softplus = jnp.maximum(z, 0.0) + jnp.log(1.0 + jnp.exp(-jnp.abs(z)))
    logw = -jnp.exp(-softplus - 0.5)
    a = _sigmoid(a0_ref[...] + la_ref[...])
    if has_vres:
        v = v + (vf_ref[0] - v) * _sigmoid(v0_ref[...] + lv_ref[...])
    kk = k * kk_ref[...]
    kk = kk / jnp.maximum(jnp.sqrt(segsum(kk * kk)), 1e-12)
    k = k * (1.0 + (a - 1.0) * ka_ref[...])

    h1 = logw.astype(BF16)
    rem = logw - h1.astype(F32)
    h2 = rem.astype(BF16)
    h3 = (rem - h2.astype(F32)).astype(BF16)
    cum = _dot(tri_ones, h1) + _dot(tri_ones, h2) + _dot(tri_ones, h3)
    w_in = jnp.exp(cum)
    w_inv = jnp.exp(-cum)
    a_t = -kk * jnp.exp(cum - logw)
    b_t = kk * a * w_inv
    k_t = k * w_inv
    r_t = r * w_in

    def c3d(x):
        return x.reshape(G, L, W)

    def per_head(x, masked):
        x3 = c3d(x)
        if masked:
            return jnp.concatenate([x3 * hm for hm in head_masks], axis=0).astype(BF16)
        return jnp.concatenate([x3.astype(BF16)] * NH, axis=0)

    def head_sum(x):
        out = x[:G]
        for e in range(1, NH):
            out = out + x[e * G:(e + 1) * G]
        return out

    a_e = per_head(a_t, True)
    r_e = per_head(r_t, True)
    v_e = per_head(v, True)
    b_2 = per_head(b_t, False)
    k_2 = per_head(k_t, False)

    x_ab = jnp.where(low_strict, _bdot_nt(a_e, b_2), 0.0)
    x_ak = jnp.where(low_strict, _bdot_nt(a_e, k_2), 0.0).astype(BF16)
    x_rb = jnp.where(low_incl, _bdot_nt(r_e, b_2), 0.0).astype(BF16)
    x_rk = jnp.where(low_incl, _bdot_nt(r_e, k_2), 0.0).astype(BF16)

    xp = jnp.where(diag_blk, x_ab, 0.0)
    tm = eye_l + xp
    for _ in range(2):
        xb = xp.astype(BF16)
        xp = _bdot(xb, xb)
        tm = tm + _bdot(tm.astype(BF16), xp.astype(BF16))
    for off_blk in merge_blks:
        tb = tm.astype(BF16)
        x_off = jnp.where(off_blk, x_ab, 0.0).astype(BF16)
        tm = tm + _bdot(_bdot(tb, x_off).astype(BF16), tb)
    tb = tm.astype(BF16)

    p_e = _bdot(tb, a_e)
    q_e = _bdot(_bdot(tb, x_ak).astype(BF16), v_e)
    p_eb = p_e.astype(BF16)
    q_eb = q_e.astype(BF16)
    rp_e = _bdot(x_rb, p_eb)
    y0_e = _bdot(x_rb, q_eb) + _bdot(x_rk, v_e)
    p_sum = head_sum(p_e)
    q_sum = head_sum(q_e)
    rp = (c3d(r_t) + head_sum(rp_e)).astype(BF16)
    y0 = head_sum(y0_e)

    b_3 = b_2[:G]
    k_3 = k_2[:G]
    w_last = c3d(w_in)[:, L - 1:L, :]
    m_lr = (jnp.where(same_head[None], _bdot_tn(p_sum.astype(BF16), b_3), 0.0) * w_last).astype(BF16)
    qv = jnp.concatenate([q_sum, c3d(v)], axis=1).astype(BF16)
    bk = jnp.concatenate([b_3, k_3], axis=1)
    c_bd = jnp.where(same_head[None], _bdot_tn(qv, bk), 0.0) * w_last

    s = s_scr[...]
    ys = []
    for c in range(G):
        s_hi, s_lo = _split2(s)
        ys.append(_dot_nt(rp[c], s_hi) + _dot_nt(rp[c], s_lo) + y0[c])
        s = s * w_last[c] + _dot(s_hi, m_lr[c]) + _dot(s_lo, m_lr[c]) + c_bd[c]
    s_scr[...] = s
    y = jnp.concatenate(ys, axis=0)

    mean = segsum(y) * (1.0 / A_HEAD)
    dlt = y - mean
    var = segsum(dlt * dlt) * (1.0 / A_HEAD)
    yn = dlt * lax.rsqrt(var + A_GN_EPS) * lnw_ref[...] + lnb_ref[...]
    bonus = segsum(r * k * rk_ref[...]) * v
    o_ref[...] = ((yn + bonus) * lg_ref[...]).astype(BF16)


def _rwkv_chunks(rkv, lw, la, lg, lv, rkv_first, w0, a0, k_k, k_a, r_k, ln_w, ln_b, v0):
    _, t, d = rkv.shape
    n_chunks = 8
    tm = n_chunks * CHUNK
    has_vres = lv is not None
    blk = lambda p, c: (c, p)
    prm = lambda p, c: (0, p)
    w = RWKV_LANES
    slab = lambda j: pl.BlockSpec((1, tm, w), lambda p, c: (j, c, p))
    in_specs = [slab(0), slab(1), slab(2)] + [pl.BlockSpec((tm, w), blk)] * (4 if has_vres else 3)
    if has_vres:
        in_specs += [slab(2)]
    in_specs += [pl.BlockSpec((1, w), prm)] * (8 if has_vres else 7)
    args = [rkv, rkv, rkv, lw, la, lg]
    if has_vres:
        args += [lv, rkv_first]
    args += [w0, a0, k_k, k_a, r_k, ln_w, ln_b]
    if has_vres:
        args += [v0]
    return pl.pallas_call(
        functools.partial(_rwkv_chunk_kernel, has_vres, n_chunks),
        grid=(d // w, t // tm),
        in_specs=in_specs,
        out_specs=pl.BlockSpec((tm, w), blk),
        out_shape=jax.ShapeDtypeStruct((t, d), BF16),
        scratch_shapes=[pltpu.VMEM((w, w), F32)],
        compiler_params=_cparams(("arbitrary", "arbitrary")),
        name="rwkv_chunks",
    )(*args)


def _dsa_split_kernel(p_ref, kvn_ref, kin_ref, q_ref, qi_ref, ckv_ref, kidx_ref, wi_ref):
    tm = p_ref.shape[0]
    nb = tm // Q_BLOCK
    o = 0
    for b in range(nb):
        rows = slice(b * Q_BLOCK, (b + 1) * Q_BLOCK)
        for h in range(B_HEADS):
            q_ref[b, h] = p_ref[rows, o + h * B_HEAD_DIM:o + (h + 1) * B_HEAD_DIM].astype(BF16)
    o += B_Q_W
    ckv = p_ref[:, o:o + B_KV_RANK]
    ckv = ckv * lax.rsqrt(jnp.mean(ckv * ckv, axis=-1, keepdims=True) + RMS_EPS) * kvn_ref[...]
    ckv_ref[...] = ckv.astype(BF16)
    o += B_KV_RANK
    for b in range(nb):
        rows = slice(b * Q_BLOCK, (b + 1) * Q_BLOCK)
        for h in range(B_IDX_HEADS):
            qi_ref[b, h] = p_ref[rows, o + h * B_IDX_DIM:o + (h + 1) * B_IDX_DIM].astype(BF16)
    o += B_QI_W
    kidx = p_ref[:, o:o + B_IDX_DIM]
    kidx = kidx * lax.rsqrt(jnp.mean(kidx * kidx, axis=-1, keepdims=True) + RMS_EPS) * kin_ref[...]
    kidx_ref[...] = kidx.astype(BF16)
    o += B_IDX_DIM
    wi_ref[...] = p_ref[:, o:o + LANES] * B_IDX_W_SCALE


def _dsa_split(proj, kv_norm, kidx_norm):
    t, n = proj.shape
    tm = 256
    nb = tm // Q_BLOCK
    return pl.pallas_call(
        _dsa_split_kernel,
        grid=(t // tm,),
        in_specs=[
            pl.BlockSpec((tm, n), lambda i: (i, 0)),
            pl.BlockSpec((1, B_KV_RANK), lambda i: (0, 0)),
            pl.BlockSpec((1, B_IDX_DIM), lambda i: (0, 0)),
        ],
        out_specs=[
            pl.BlockSpec((nb, B_HEADS, Q_BLOCK, B_HEAD_DIM), lambda i: (i, 0, 0, 0)),
            pl.BlockSpec((nb, B_IDX_HEADS, Q_BLOCK, B_IDX_DIM), lambda i: (i, 0, 0, 0)),
            pl.BlockSpec((tm, B_KV_RANK), lambda i: (i, 0)),
            pl.BlockSpec((tm, B_IDX_DIM), lambda i: (i, 0)),
            pl.BlockSpec((tm, LANES), lambda i: (i, 0)),
        ],
        out_shape=[
            jax.ShapeDtypeStruct((t // Q_BLOCK, B_HEADS, Q_BLOCK, B_HEAD_DIM), BF16),
            jax.ShapeDtypeStruct((t // Q_BLOCK, B_IDX_HEADS, Q_BLOCK, B_IDX_DIM), BF16),
            jax.ShapeDtypeStruct((t, B_KV_RANK), BF16),
            jax.ShapeDtypeStruct((t, B_IDX_DIM), BF16),
            jax.ShapeDtypeStruct((t, LANES), F32),
        ],
        compiler_params=_cparams(("arbitrary",)),
        name="dsa_split",
    )(proj, kv_norm, kidx_norm)


def _dsa_attn_kernel(topk, tk, idx_bits,
                     q_ref, qi_ref, wrow_ref, kidx_ref, ckv_ref, ckvt_ref, wuk_ref, wuv_ref, o_ref,
                     key_scr, qlat_scr, acc_scr, m_scr, l_scr, j0_scr):
    i = pl.program_id(0)
    QB = Q_BLOCK
    cw = ATT_HEADS_PER_STEP * QB
    n_kt = (i * QB + QB + tk - 1) // tk
    qpos = i * QB + lax.broadcasted_iota(I32, (1, QB), 1)
    limit = (qpos // CHUNK + 1) * CHUNK

    def key_pos(off):
        return off + lax.broadcasted_iota(I32, (tk, QB), 0)

    for h in range(B_HEADS):
        ql = _dot(q_ref[0, h], wuk_ref[h]) * (B_SCALE * LOG2_E)
        qlat_scr[h * QB:(h + 1) * QB, :] = ql.astype(BF16)

    qi = qi_ref[0].reshape(B_IDX_HEADS * QB, B_IDX_DIM)
    wrow = wrow_ref[0]

    def score_body(kt, carry):
        off = pl.multiple_of(kt * tk, tk)
        rel = jnp.maximum(_dot_nt(kidx_ref[pl.ds(off, tk), :], qi), 0.0) * wrow
        sc = rel[:, 0:QB]
        for h in range(1, B_IDX_HEADS):
            sc = sc + rel[:, h * QB:(h + 1) * QB]
        sc = jnp.where(sc == 0.0, 0.0, sc)
        bits = pltpu.bitcast(sc, I32)
        key = bits ^ ((bits >> 31) & INT_MAX)
        key_scr[pl.ds(off, tk), :] = jnp.where(key_pos(off) < limit, key, INT_MIN)
        return carry

    lax.fori_loop(0, n_kt, score_body, 0)

    def count_keys(pred):
        def body(kt, acc):
            off = pl.multiple_of(kt * tk, tk)
            hit = jnp.where(pred(key_scr[pl.ds(off, tk), :], off), 1.0, 0.0)
            return acc + jnp.sum(hit.reshape(tk // (8 * COUNT_WAYS), COUNT_WAYS, 8, QB), axis=0)
        acc = lax.fori_loop(0, n_kt, body, jnp.zeros((COUNT_WAYS, 8, QB), F32))
        return jnp.sum(jnp.sum(acc, axis=0), axis=0, keepdims=True)

    def bit_body(b, thr_u):
        cand = thr_u | jnp.left_shift(jnp.int32(1), 31 - b)
        cand_s = cand ^ INT_MIN
        cnt = count_keys(lambda key, off: key >= cand_s)
        return jnp.where(cnt >= topk, cand, thr_u)

    thr = lax.fori_loop(0, 32, bit_body, jnp.zeros((1, QB), I32)) ^ INT_MIN
    cnt_gt = count_keys(lambda key, off: key > thr)
    cnt_ge = count_keys(lambda key, off: key >= thr)
    need = topk - cnt_gt

    j0_scr[...] = jnp.full((1, QB), INT_MAX, I32)
    excess = jnp.max(jnp.where(thr > INT_MIN, cnt_ge, 0.0)) > topk

    @pl.when(excess)
    def _():
        def tie_body(b, j0):
            cand = j0 | jnp.left_shift(jnp.int32(1), idx_bits - 1 - b)
            cnt = count_keys(lambda key, off: (key == thr) & (key_pos(off) < cand))
            return jnp.where(cnt < need, cand, j0)
        j0_scr[...] = lax.fori_loop(0, idx_bits, tie_body, jnp.zeros((1, QB), I32))

    j0 = j0_scr[...]

    m_scr[...] = jnp.full(m_scr.shape, NEG_BIG, F32)
    l_scr[...] = jnp.zeros_like(l_scr)
    acc_scr[...] = jnp.zeros_like(acc_scr)

    def att_body(kt, carry):
        off = pl.multiple_of(kt * tk, tk)
        key = key_scr[pl.ds(off, tk), :]
        sel = ((key > thr) | ((key == thr) & (key_pos(off) <= j0))) & (key > INT_MIN)
        bias = jnp.where(sel, 0.0, NEG_BIG)
        bias = jnp.concatenate([bias] * ATT_HEADS_PER_STEP, axis=1)
        ckv_t = ckv_ref[pl.ds(off, tk), :]
        ckvt_t = ckvt_ref[:, pl.ds(off, tk)]
        n_groups = B_HEADS // ATT_HEADS_PER_STEP

        def logits(c):
            return _dot_nt(ckv_t, qlat_scr[c * cw:(c + 1) * cw, :]) + bias

        s_next = logits(0)
        for c in range(n_groups):
            cols = slice(c * cw, (c + 1) * cw)
            s = s_next
            if c + 1 < n_groups:
                s_next = logits(c + 1)
            m_old = m_scr[:, cols]
            m_new = jnp.maximum(m_old, jnp.max(s, axis=0, keepdims=True))
            alpha = jnp.exp2(m_old - m_new)
            p = jnp.exp2(s - m_new)
            l_scr[:, cols] = alpha * l_scr[:, cols] + jnp.sum(p, axis=0, keepdims=True)
            acc_scr[:, cols] = alpha * acc_scr[:, cols] + _dot(ckvt_t, p.astype(BF16))
            m_scr[:, cols] = m_new
        return carry

    lax.fori_loop(0, n_kt, att_body, 0)

    o_lat_t = (acc_scr[...] / l_scr[...]).astype(BF16)
    for h in range(B_HEADS):
        oh = _dot_tn(o_lat_t[:, h * QB:(h + 1) * QB], wuv_ref[h])
        o_ref[:, h * B_HEAD_DIM:(h + 1) * B_HEAD_DIM] = oh.astype(BF16)


def _dsa_attn(q, qi, wrow, kidx, ckv, ckvt, w_uk, w_uv):
    nb = q.shape[0]
    t = kidx.shape[0]
    tk = 512
    topk = min(TOPK_MAX, t // 4)
    idx_bits = max(1, (t - 1).bit_length())
    const2 = lambda i: (0, 0)
    const3 = lambda i: (0, 0, 0)
    hq = B_HEADS * Q_BLOCK
    once = pl.Buffered(1)
    return pl.pallas_call(
        functools.partial(_dsa_attn_kernel, topk, tk, idx_bits),
        grid=(nb,),
        in_specs=[
            pl.BlockSpec((1, B_HEADS, Q_BLOCK, B_HEAD_DIM), lambda i: (i, 0, 0, 0)),
            pl.BlockSpec((1, B_IDX_HEADS, Q_BLOCK, B_IDX_DIM), lambda i: (i, 0, 0, 0)),
            pl.BlockSpec((1, 1, B_IDX_HEADS * Q_BLOCK), lambda i: (i, 0, 0)),
            pl.BlockSpec((t, B_IDX_DIM), const2, pipeline_mode=once),
            pl.BlockSpec((t, B_KV_RANK), const2, pipeline_mode=once),
            pl.BlockSpec((B_KV_RANK, t), const2, pipeline_mode=once),
            pl.BlockSpec((B_HEADS, B_HEAD_DIM, B_KV_RANK), const3, pipeline_mode=once),
            pl.BlockSpec((B_HEADS, B_KV_RANK, B_HEAD_DIM), const3, pipeline_mode=once),
        ],
        out_specs=pl.BlockSpec((Q_BLOCK, B_Q_W), lambda i: (i, 0)),
        out_shape=jax.ShapeDtypeStruct((t, B_Q_W), BF16),
        scratch_shapes=[
            pltpu.VMEM((t, Q_BLOCK), I32),
            pltpu.VMEM((hq, B_KV_RANK), BF16),
            pltpu.VMEM((B_KV_RANK, hq), F32),
            pltpu.VMEM((1, hq), F32),
            pltpu.VMEM((1, hq), F32),
            pltpu.VMEM((1, Q_BLOCK), I32),
        ],
        compiler_params=_cparams(("arbitrary",)),
        name="dsa_attn",
    )(q, qi, wrow, kidx, ckv, ckvt, w_uk, w_uv)


def _final_norm_kernel(x_ref, w_ref, o_ref):
    x = x_ref[...]
    o_ref[...] = x * lax.rsqrt(jnp.mean(x * x, axis=-1, keepdims=True) + RMS_EPS) * w_ref[...]


def _final_norm(x, w):
    t, d = x.shape
    tm = 512
    return pl.pallas_call(
        _final_norm_kernel,
        grid=(t // tm,),
        in_specs=[pl.BlockSpec((tm, d), lambda i: (i, 0)), pl.BlockSpec((1, d), lambda i: (0, 0))],
        out_specs=pl.BlockSpec((tm, d), lambda i: (i, 0)),
        out_shape=jax.ShapeDtypeStruct((t, d), F32),
        compiler_params=_cparams(("arbitrary",)),
        name="final_norm",
    )(x, w)


def _row(v):
    return v.reshape(1, -1)


def _rwkv_layer(x, shift, scale, gate, rkv_first, mu, w_rkv, w0, w1, w2, a0, a1, a2,
                g1, g2, k_k, k_a, r_k, ln_w, ln_b, w_o, vres):
    mixes = _rwkv_prep(x, shift, scale, mu[jnp.array([0, 2, 3, 1, 4, 5])])
    rkv = _bmm3(mixes, w_rkv.astype(BF16))
    lw = _lora(mixes, 3, w1.astype(BF16), w2.astype(BF16), "tanh")
    la = _lora(mixes, 4, a1.astype(BF16), a2.astype(BF16), "none")
    lg = _lora(mixes, 5, g1.astype(BF16), g2.astype(BF16), "sigmoid")
    if vres is None:
        lv, v0 = None, None
        rkv_first = rkv
    else:
        v0, v1, v2 = vres
        lv = _lora(mixes, 2, v1.astype(BF16), v2.astype(BF16), "none")
        v0 = _row(v0)
    y = _rwkv_chunks(rkv, lw, la, lg, lv, rkv_first, _row(w0), _row(a0), _row(k_k), _row(k_a),
                     _row(r_k), _row(ln_w), _row(ln_b), v0)
    return _proj_res(y, w_o.astype(BF16), x, gate), rkv_first


def _dsa_layer(x, shift, scale, gate, w_in, kv_norm, kidx_norm, w_uk, w_uv, w_o):
    t = x.shape[0]
    n_in = w_in.shape[1]
    n_pad = -(-(n_in - B_IDX_HEADS + LANES) // 768) * 768
    w_in_p = jnp.pad(w_in, ((0, 0), (0, n_pad - n_in))).astype(BF16)
    proj = _norm_proj(x, shift, scale, w_in_p, 768)
    q, qi, ckv, kidx, wi = _dsa_split(proj, _row(kv_norm), _row(kidx_norm))
    nb = t // Q_BLOCK
    wrow = wi[:, :B_IDX_HEADS].reshape(nb, Q_BLOCK, B_IDX_HEADS).transpose(0, 2, 1)
    wrow = wrow.reshape(nb, 1, B_IDX_HEADS * Q_BLOCK)
    o = _dsa_attn(q, qi, wrow, kidx, ckv, ckv.T, w_uk.astype(BF16), w_uv.astype(BF16))
    return _proj_res(o, w_o.astype(BF16), x, gate)


def kernel(x, c, ada_w, ada_b, a_mu, a_w_rkv, a_w0, a_w1, a_w2, a_a0, a_a1, a_a2, a_v0, a_v1, a_v2, a_g1, a_g2, a_k_k, a_k_a, a_r_k, a_ln_w, a_ln_b, a_w_o, b_w_in, b_kv_norm, b_kidx_norm, b_w_uk, b_w_uv, b_w_o, f_w1, f_w3, f_w2, final_norm):
    b, t, d = x.shape
    assert b == 1 and d == D_MODEL
    mod_all = _ada(c, ada_w, ada_b)
    xs = x.reshape(t, d)
    v_first = None
    for i in range(DEPTH):
        shift1, scale1, gate1, shift2, scale2, gate2 = (
            mod_all[i, :, s * d:(s + 1) * d] for s in range(6))
        j = i // 2
        if i % 2 == 0:
            vres = None if j == 0 else (a_v0[j - 1], a_v1[j - 1], a_v2[j - 1])
            xs, v_first = _rwkv_layer(
                xs, shift1, scale1, gate1, v_first, a_mu[j], a_w_rkv[j], a_w0[j], a_w1[j], a_w2[j],
                a_a0[j], a_a1[j], a_a2[j], a_g1[j], a_g2[j], a_k_k[j], a_k_a[j], a_r_k[j],
                a_ln_w[j], a_ln_b[j], a_w_o[j], vres)
        else:
            xs = _dsa_layer(xs, shift1, scale1, gate1, b_w_in[j], b_kv_norm[j], b_kidx_norm[j],
                            b_w_uk[j], b_w_uv[j], b_w_o[j])
        xs = _ffn(xs, shift2, scale2, gate2, f_w1[i].astype(BF16), f_w3[i].astype(BF16),
                  f_w2[i].astype(BF16))
    return _final_norm(xs, _row(final_norm)).reshape(b, t, d)
```

```python
import functools

import jax
import jax.numpy as jnp
from jax import lax
from jax.experimental import pallas as pl
from jax.experimental.pallas import tpu as pltpu

F32 = jnp.float32
BF16 = jnp.bfloat16
I32 = jnp.int32

D_MODEL = 2048
DEPTH = 4
CHUNK = 64
RMS_EPS = 1e-6
A_HEAD = 64
A_HEADS = D_MODEL // A_HEAD
A_GN_EPS = A_HEAD * 1e-5
B_HEADS = 16
B_HEAD_DIM = 128
B_KV_RANK = 512
B_IDX_HEADS = 8
B_IDX_DIM = 128
TOPK_MAX = 256
Q_BLOCK = 128
B_Q_W = B_HEADS * B_HEAD_DIM
B_QI_W = B_IDX_HEADS * B_IDX_DIM
B_SCALE = B_HEAD_DIM ** -0.5
B_IDX_W_SCALE = (B_IDX_HEADS ** -0.5) * (B_IDX_DIM ** -0.5)
D_FF = 5632

LANES = 128
VMEM_LIMIT = 56 * 1024 * 1024
INT_MIN = -(2 ** 31)
INT_MAX = 2 ** 31 - 1
NEG_BIG = -1e30
LOG2_E = 1.4426950408889634
COUNT_WAYS = 8
ATT_HEADS_PER_STEP = 4
RWKV_LANES = 256

_NT = (((1,), (1,)), ((), ()))
_TN = (((0,), (0,)), ((), ()))


def _cparams(sem):
    return pltpu.CompilerParams(dimension_semantics=sem, vmem_limit_bytes=VMEM_LIMIT)


def _dot(a, b):
    return jnp.dot(a, b, preferred_element_type=F32)


def _dot_nt(a, b):
    return lax.dot_general(a, b, _NT, preferred_element_type=F32)


def _dot_tn(a, b):
    return lax.dot_general(a, b, _TN, preferred_element_type=F32)


def _modulate(x, shift, scale):
    ms = jnp.mean(x * x, axis=-1, keepdims=True)
    return x * lax.rsqrt(ms + RMS_EPS) * (1.0 + scale) + shift


def _sigmoid(x):
    return 1.0 / (1.0 + jnp.exp(-x))


def _ada_kernel(c_ref, w_ref, b_ref, o_ref):
    c = c_ref[...]
    s = c * _sigmoid(c)
    o_ref[0] = jnp.sum(s * w_ref[0], axis=0, keepdims=True) + b_ref[0]


def _ada(c, ada_w, ada_b):
    depth, d, n = ada_w.shape
    tn = 1024
    return pl.pallas_call(
        _ada_kernel,
        grid=(depth, n // tn),
        in_specs=[
            pl.BlockSpec((d, 1), lambda l, j: (0, 0)),
            pl.BlockSpec((1, d, tn), lambda l, j: (l, 0, j)),
            pl.BlockSpec((1, 1, tn), lambda l, j: (l, 0, j)),
        ],
        out_specs=pl.BlockSpec((1, 1, tn), lambda l, j: (l, 0, j)),
        out_shape=jax.ShapeDtypeStruct((depth, 1, n), F32),
        compiler_params=_cparams(("arbitrary", "arbitrary")),
        name="ada_mod",
    )(c.reshape(d, 1), ada_w, ada_b.reshape(depth, 1, n))


def _ffn_kernel(x_ref, sh_ref, sc_ref, g_ref, w1_ref, w3_ref, w2_ref, o_ref, h_scr, acc_scr):
    f = pl.program_id(1)

    @pl.when(f == 0)
    def _():
        h_scr[...] = _modulate(x_ref[...], sh_ref[...], sc_ref[...]).astype(BF16)
        acc_scr[...] = jnp.zeros_like(acc_scr)

    h = h_scr[...]
    a = _dot(h, w1_ref[...])
    b = _dot(h, w3_ref[...])
    u = (a * _sigmoid(a) * b).astype(BF16)
    acc_scr[...] += _dot(u, w2_ref[...])

    @pl.when(f == pl.num_programs(1) - 1)
    def _():
        o_ref[...] = x_ref[...] + g_ref[...] * acc_scr[...]


def _ffn(x, shift, scale, gate, w1, w3, w2):
    t, d = x.shape
    dff = w1.shape[1]
    tm, tf = 512, 512
    row = lambda i, f: (0, 0)
    return pl.pallas_call(
        _ffn_kernel,
        grid=(t // tm, dff // tf),
        in_specs=[
            pl.BlockSpec((tm, d), lambda i, f: (i, 0)),
            pl.BlockSpec((1, d), row),
            pl.BlockSpec((1, d), row),
            pl.BlockSpec((1, d), row),
            pl.BlockSpec((d, tf), lambda i, f: (0, f)),
            pl.BlockSpec((d, tf), lambda i, f: (0, f)),
            pl.BlockSpec((tf, d), lambda i, f: (f, 0)),
        ],
        out_specs=pl.BlockSpec((tm, d), lambda i, f: (i, 0)),
        out_shape=jax.ShapeDtypeStruct((t, d), F32),
        scratch_shapes=[pltpu.VMEM((tm, d), BF16), pltpu.VMEM((tm, d), F32)],
        compiler_params=_cparams(("arbitrary", "arbitrary")),
        name="ffn",
    )(x, shift, scale, gate, w1, w3, w2)


def _proj_res_kernel(a_ref, w_ref, x_ref, g_ref, o_ref):
    o_ref[...] = x_ref[...] + g_ref[...] * _dot(a_ref[...], w_ref[...])


def _proj_res(a, w, x, gate):
    t, k = a.shape
    n = w.shape[1]
    tm, tn = 1024, 512
    return pl.pallas_call(
        _proj_res_kernel,
        grid=(t // tm, n // tn),
        in_specs=[
            pl.BlockSpec((tm, k), lambda i, j: (i, 0)),
            pl.BlockSpec((k, tn), lambda i, j: (0, j)),
            pl.BlockSpec((tm, tn), lambda i, j: (i, j)),
            pl.BlockSpec((1, tn), lambda i, j: (0, j)),
        ],
        out_specs=pl.BlockSpec((tm, tn), lambda i, j: (i, j)),
        out_shape=jax.ShapeDtypeStruct((t, n), F32),
        compiler_params=_cparams(("arbitrary", "arbitrary")),
        name="proj_res",
    )(a, w, x, gate)


def _norm_proj_kernel(x_ref, sh_ref, sc_ref, w_ref, o_ref, h_scr):
    @pl.when(pl.program_id(1) == 0)
    def _():
        h_scr[...] = _modulate(x_ref[...], sh_ref[...], sc_ref[...]).astype(BF16)

    o_ref[...] = _dot(h_scr[...], w_ref[...])


def _norm_proj(x, shift, scale, w, tn):
    t, d = x.shape
    n = w.shape[1]
    tm = 512
    row = lambda i, j: (0, 0)
    return pl.pallas_call(
        _norm_proj_kernel,
        grid=(t // tm, n // tn),
        in_specs=[
            pl.BlockSpec((tm, d), lambda i, j: (i, 0)),
            pl.BlockSpec((1, d), row),
            pl.BlockSpec((1, d), row),
            pl.BlockSpec((d, tn), lambda i, j: (0, j)),
        ],
        out_specs=pl.BlockSpec((tm, tn), lambda i, j: (i, j)),
        out_shape=jax.ShapeDtypeStruct((t, n), F32),
        scratch_shapes=[pltpu.VMEM((tm, d), BF16)],
        compiler_params=_cparams(("arbitrary", "arbitrary")),
        name="norm_proj",
    )(x, shift, scale, w)


def _rwkv_front_kernel(has_vres, x_ref, sh_ref, sc_ref, mu_ref, wrkv_ref, w1_ref, a1_ref, g1_ref,
                       *rest):
    if has_vres:
        v1_ref, rkv_ref, zw_ref, za_ref, zg_ref, zv_ref, carry = rest
    else:
        rkv_ref, zw_ref, za_ref, zg_ref, carry = rest
    i = pl.program_id(0)
    tm = x_ref.shape[0]

    @pl.when(i == 0)
    def _():
        carry[...] = jnp.zeros_like(carry)

    h = _modulate(x_ref[...], sh_ref[...], sc_ref[...])
    prev_last = carry[7:8, :]
    rolled = pltpu.roll(h, 1, axis=0)
    rows = lax.broadcasted_iota(I32, h.shape, 0)
    h_prev = jnp.where(rows == 0, prev_last, rolled)
    carry[...] = h[tm - 8:tm, :]
    dx = h_prev - h

    def mix(j):
        return (h + dx * mu_ref[j:j + 1, :]).astype(BF16)

    xv = mix(3)
    rkv_ref[0] = _dot(mix(0), wrkv_ref[0])
    rkv_ref[1] = _dot(mix(2), wrkv_ref[1])
    rkv_ref[2] = _dot(xv, wrkv_ref[2])
    zw_ref[...] = jnp.tanh(_dot(mix(1), w1_ref[...])).astype(BF16)
    za_ref[...] = _dot(mix(4), a1_ref[...]).astype(BF16)
    zg_ref[...] = _sigmoid(_dot(mix(5), g1_ref[...])).astype(BF16)
    if has_vres:
        zv_ref[...] = _dot(xv, v1_ref[...]).astype(BF16)


def _pad_cols(w):
    r = w.shape[1]
    return jnp.pad(w, ((0, 0), (0, -(-r // LANES) * LANES - r))).astype(BF16)


def _pad_rows(w):
    r = w.shape[0]
    return jnp.pad(w, ((0, -(-r // LANES) * LANES - r), (0, 0))).astype(BF16)


def _rwkv_front(x, shift, scale, mu, w_rkv, w1, a1, g1, v1):
    t, d = x.shape
    tm = 256
    has_vres = v1 is not None
    once = pl.Buffered(1)
    lora_ws = [w1, a1, g1] + ([v1] if has_vres else [])
    const2 = lambda i: (0, 0)
    in_specs = [
        pl.BlockSpec((tm, d), lambda i: (i, 0)),
        pl.BlockSpec((1, d), const2),
        pl.BlockSpec((1, d), const2),
        pl.BlockSpec((6, d), const2),
        pl.BlockSpec((3, d, d), lambda i: (0, 0, 0), pipeline_mode=once),
    ] + [pl.BlockSpec(w.shape, const2, pipeline_mode=once) for w in lora_ws]
    out_specs = [pl.BlockSpec((3, tm, d), lambda i: (0, i, 0))]
    out_specs += [pl.BlockSpec((tm, w.shape[1]), lambda i: (i, 0)) for w in lora_ws]
    out_shape = [jax.ShapeDtypeStruct((3, t, d), F32)]
    out_shape += [jax.ShapeDtypeStruct((t, w.shape[1]), BF16) for w in lora_ws]
    return pl.pallas_call(
        functools.partial(_rwkv_front_kernel, has_vres),
        grid=(t // tm,),
        in_specs=in_specs,
        out_specs=out_specs,
        out_shape=out_shape,
        scratch_shapes=[pltpu.VMEM((8, d), F32)],
        compiler_params=_cparams(("arbitrary",)),
        name="rwkv_front",
    )(x, shift, scale, mu, w_rkv, *lora_ws)


def _split2(x):
    hi = x.astype(BF16)
    lo = (x - hi.astype(F32)).astype(BF16)
    return hi, lo


def _bdot(a, b):
    return lax.dot_general(a, b, (((2,), (1,)), ((0,), (0,))), preferred_element_type=F32)


def _bdot_nt(a, b):
    return lax.dot_general(a, b, (((2,), (2,)), ((0,), (0,))), preferred_element_type=F32)


def _bdot_tn(a, b):
    return lax.dot_general(a, b, (((1,), (1,)), ((0,), (0,))), preferred_element_type=F32)


def _rwkv_chunk_kernel(has_vres, n_chunks, *refs):
    if has_vres:
        (r_ref, k_ref, v_ref, zw_ref, za_ref, zg_ref, zv_ref, vf_ref,
         w2_ref, a2_ref, g2_ref, v2_ref,
         w0_ref, a0_ref, kk_ref, ka_ref, rk_ref, lnw_ref, lnb_ref, v0_ref,
         o_ref, s_scr) = refs
    else:
        (r_ref, k_ref, v_ref, zw_ref, za_ref, zg_ref,
         w2_ref, a2_ref, g2_ref,
         w0_ref, a0_ref, kk_ref, ka_ref, rk_ref, lnw_ref, lnb_ref,
         o_ref, s_scr) = refs

    L = CHUNK
    W = o_ref.shape[1]
    NH = W // A_HEAD
    G = n_chunks
    R = G * L

    @pl.when(pl.program_id(1) == 0)
    def _():
        s_scr[...] = jnp.zeros_like(s_scr)

    lane = lax.broadcasted_iota(I32, (1, W), 1)
    head_masks = [((lane // A_HEAD) == e).astype(F32) for e in range(NH)]
    rr = lax.broadcasted_iota(I32, (L, L), 0)
    cc = lax.broadcasted_iota(I32, (L, L), 1)
    low_strict = (rr > cc)[None]
    low_incl = (rr >= cc)[None]
    eye_l = (rr == cc).astype(F32)[None]
    diag_blk = ((rr >> 3) == (cc >> 3))[None]
    merge_blks = [(((rr >> (s + 1)) == (cc >> (s + 1))) & ((rr >> s) > (cc >> s)))[None]
                  for s in (3, 4, 5)]
    r2 = lax.broadcasted_iota(I32, (W, W), 0)
    c2 = lax.broadcasted_iota(I32, (W, W), 1)
    same_head = (r2 // A_HEAD) == (c2 // A_HEAD)
    seg_ones = same_head.astype(BF16)
    r3 = lax.broadcasted_iota(I32, (R, R), 0)
    c3 = lax.broadcasted_iota(I32, (R, R), 1)
    tri_ones = (((r3 // L) == (c3 // L)) & (r3 >= c3)).astype(BF16)

    def segsum(x):
        hi, lo = _split2(x)
        return _dot(hi, seg_ones) + _dot(lo, seg_ones)

    r = r_ref[0]
    k = k_ref[0]
    v = v_ref[0]
    z = -(w0_ref[...] + _dot(zw_ref[...], w2_ref[...]))
    softplus = jnp.maximum(z, 0.0) + jnp.log(1.0 + jnp.exp(-jnp.abs(z)))
    logw = -jnp.exp(-softplus - 0.5)
    a = _sigmoid(a0_ref[...] + _dot(za_ref[...], a2_ref[...]))
    if has_vres:
        v = v + (vf_ref[0] - v) * _sigmoid(v0_ref[...] + _dot(zv_ref[...], v2_ref[...]))
    kk = k * kk_ref[...]
    kk = kk / jnp.maximum(jnp.sqrt(segsum(kk * kk)), 1e-12)
    k = k * (1.0 + (a - 1.0) * ka_ref[...])

    h1 = logw.astype(BF16)
    rem = logw - h1.astype(F32)
    h2 = rem.astype(BF16)
    h3 = (rem - h2.astype(F32)).astype(BF16)
    cum = _dot(tri_ones, h1) + _dot(tri_ones, h2) + _dot(tri_ones, h3)
    w_in = jnp.exp(cum)
    w_inv = jnp.exp(-cum)
    a_t = -kk * jnp.exp(cum - logw)
    b_t = kk * a * w_inv
    k_t = k * w_inv
    r_t = r * w_in

    def c3d(x):
        return x.reshape(G, L, W)

    def per_head(x, masked):
        x3 = c3d(x)
        if masked:
            return jnp.concatenate([x3 * hm for hm in head_masks], axis=0).astype(BF16)
        return jnp.concatenate([x3.astype(BF16)] * NH, axis=0)

    def head_sum(x):
        out = x[:G]
        for e in range(1, NH):
            out = out + x[e * G:(e + 1) * G]
        return out

    a_e = per_head(a_t, True)
    r_e = per_head(r_t, True)
    v_e = per_head(v, True)
    b_2 = per_head(b_t, False)
    k_2 = per_head(k_t, False)

    x_ab = jnp.where(low_strict, _bdot_nt(a_e, b_2), 0.0)
    x_ak = jnp.where(low_strict, _bdot_nt(a_e, k_2), 0.0).astype(BF16)
    x_rb = jnp.where(low_incl, _bdot_nt(r_e, b_2), 0.0).astype(BF16)
    x_rk = jnp.where(low_incl, _bdot_nt(r_e, k_2), 0.0).astype(BF16)

    xp = jnp.where(diag_blk, x_ab, 0.0)
    tm = eye_l + xp
    for _ in range(2):
        xb = xp.astype(BF16)
        xp = _bdot(xb, xb)
        tm = tm + _bdot(tm.astype(BF16), xp.astype(BF16))
    for off_blk in merge_blks:
        tb = tm.astype(BF16)
        x_off = jnp.where(off_blk, x_ab, 0.0).astype(BF16)
        tm = tm + _bdot(_bdot(tb, x_off).astype(BF16), tb)
    tb = tm.astype(BF16)

    p_e = _bdot(tb, a_e)
    q_e = _bdot(_bdot(tb, x_ak).astype(BF16), v_e)
    p_eb = p_e.astype(BF16)
    q_eb = q_e.astype(BF16)
    rp_e = _bdot(x_rb, p_eb)
    y0_e = _bdot(x_rb, q_eb) + _bdot(x_rk, v_e)
    p_sum = head_sum(p_e)
    q_sum = head_sum(q_e)
    rp = (c3d(r_t) + head_sum(rp_e)).astype(BF16)
    y0 = head_sum(y0_e)

    b_3 = b_2[:G]
    k_3 = k_2[:G]
    w_last = c3d(w_in)[:, L - 1:L, :]
    m_lr = (jnp.where(same_head[None], _bdot_tn(p_sum.astype(BF16), b_3), 0.0) * w_last).astype(BF16)
    qv = jnp.concatenate([q_sum, c3d(v)], axis=1).astype(BF16)
    bk = jnp.concatenate([b_3, k_3], axis=1)
    c_bd = jnp.where(same_head[None], _bdot_tn(qv, bk), 0.0) * w_last

    s = s_scr[...]
    ys = []
    for c in range(G):
        s_hi, s_lo = _split2(s)
        ys.append(_dot_nt(rp[c], s_hi) + _dot_nt(rp[c], s_lo) + y0[c])
        s = s * w_last[c] + _dot(s_hi, m_lr[c]) + _dot(s_lo, m_lr[c]) + c_bd[c]
    s_scr[...] = s
    y = jnp.concatenate(ys, axis=0)

    mean = segsum(y) * (1.0 / A_HEAD)
    dlt = y - mean
    var = segsum(dlt * dlt) * (1.0 / A_HEAD)
    yn = dlt * lax.rsqrt(var + A_GN_EPS) * lnw_ref[...] + lnb_ref[...]
    bonus = segsum(r * k * rk_ref[...]) * v
    o_ref[...] = ((yn + bonus) * _dot(zg_ref[...], g2_ref[...])).astype(BF16)


def _rwkv_chunks(rkv, zs, ups, rkv_first, w0, a0, k_k, k_a, r_k, ln_w, ln_b, v0):
    _, t, d = rkv.shape
    n_chunks = 8
    tm = n_chunks * CHUNK
    has_vres = len(zs) == 4
    blk = lambda p, c: (c, p)
    prm = lambda p, c: (0, p)
    w = RWKV_LANES
    slab = lambda j: pl.BlockSpec((1, tm, w), lambda p, c: (j, c, p))
    in_specs = [slab(0), slab(1), slab(2)]
    in_specs += [pl.BlockSpec((tm, z.shape[1]), lambda p, c: (c, 0)) for z in zs]
    if has_vres:
        in_specs += [slab(2)]
    in_specs += [pl.BlockSpec((u.shape[0], w), prm) for u in ups]
    in_specs += [pl.BlockSpec((1, w), prm)] * (8 if has_vres else 7)
    args = [rkv, rkv, rkv] + list(zs)
    if has_vres:
        args += [rkv_first]
    args += list(ups)
    args += [w0, a0, k_k, k_a, r_k, ln_w, ln_b]
    if has_vres:
        args += [v0]
    return pl.pallas_call(
        functools.partial(_rwkv_chunk_kernel, has_vres, n_chunks),
        grid=(d // w, t // tm),
        in_specs=in_specs,
        out_specs=pl.BlockSpec((tm, w), blk),
        out_shape=jax.ShapeDtypeStruct((t, d), BF16),
        scratch_shapes=[pltpu.VMEM((w, w), F32)],
        compiler_params=_cparams(("arbitrary", "arbitrary")),
        name="rwkv_chunks",
    )(*args)


def _dsa_split_kernel(p_ref, kvn_ref, kin_ref, q_ref, qi_ref, ckv_ref, kidx_ref, wi_ref):
    tm = p_ref.shape[0]
    nb = tm // Q_BLOCK
    o = 0
    for b in range(nb):
        rows = slice(b * Q_BLOCK, (b + 1) * Q_BLOCK)
        for h in range(B_HEADS):
            q_ref[b, h] = p_ref[rows, o + h * B_HEAD_DIM:o + (h + 1) * B_HEAD_DIM].astype(BF16)
    o += B_Q_W
    ckv = p_ref[:, o:o + B_KV_RANK]
    ckv = ckv * lax.rsqrt(jnp.mean(ckv * ckv, axis=-1, keepdims=True) + RMS_EPS) * kvn_ref[...]
    ckv_ref[...] = ckv.astype(BF16)
    o += B_KV_RANK
    for b in range(nb):
        rows = slice(b * Q_BLOCK, (b + 1) * Q_BLOCK)
        for h in range(B_IDX_HEADS):
            qi_ref[b, h] = p_ref[rows, o + h * B_IDX_DIM:o + (h + 1) * B_IDX_DIM].astype(BF16)
    o += B_QI_W
    kidx = p_ref[:, o:o + B_IDX_DIM]
    kidx = kidx * lax.rsqrt(jnp.mean(kidx * kidx, axis=-1, keepdims=True) + RMS_EPS) * kin_ref[...]
    kidx_ref[...] = kidx.astype(BF16)
    o += B_IDX_DIM
    wi_ref[...] = p_ref[:, o:o + LANES] * B_IDX_W_SCALE


def _dsa_split(proj, kv_norm, kidx_norm):
    t, n = proj.shape
    tm = 256
    nb = tm // Q_BLOCK
    return pl.pallas_call(
        _dsa_split_kernel,
        grid=(t // tm,),
        in_specs=[
            pl.BlockSpec((tm, n), lambda i: (i, 0)),
            pl.BlockSpec((1, B_KV_RANK), lambda i: (0, 0)),
            pl.BlockSpec((1, B_IDX_DIM), lambda i: (0, 0)),
        ],
        out_specs=[
            pl.BlockSpec((nb, B_HEADS, Q_BLOCK, B_HEAD_DIM), lambda i: (i, 0, 0, 0)),
            pl.BlockSpec((nb, B_IDX_HEADS, Q_BLOCK, B_IDX_DIM), lambda i: (i, 0, 0, 0)),
            pl.BlockSpec((tm, B_KV_RANK), lambda i: (i, 0)),
            pl.BlockSpec((tm, B_IDX_DIM), lambda i: (i, 0)),
            pl.BlockSpec((tm, LANES), lambda i: (i, 0)),
        ],
        out_shape=[
            jax.ShapeDtypeStruct((t // Q_BLOCK, B_HEADS, Q_BLOCK, B_HEAD_DIM), BF16),
            jax.ShapeDtypeStruct((t // Q_BLOCK, B_IDX_HEADS, Q_BLOCK, B_IDX_DIM), BF16),
            jax.ShapeDtypeStruct((t, B_KV_RANK), BF16),
            jax.ShapeDtypeStruct((t, B_IDX_DIM), BF16),
            jax.ShapeDtypeStruct((t, LANES), F32),
        ],
        compiler_params=_cparams(("arbitrary",)),
        name="dsa_split",
    )(proj, kv_norm, kidx_norm)


def _dsa_attn_kernel(topk, tk, idx_bits,
                     q_ref, qi_ref, wrow_ref, kidx_ref, ckv_ref, ckvt_ref, wuk_ref, wuv_ref, o_ref,
                     key_scr, qlat_scr, acc_scr, m_scr, l_scr, j0_scr):
    i = pl.program_id(0)
    QB = Q_BLOCK
    cw = ATT_HEADS_PER_STEP * QB
    n_kt = (i * QB + QB + tk - 1) // tk
    qpos = i * QB + lax.broadcasted_iota(I32, (1, QB), 1)
    limit = (qpos // CHUNK + 1) * CHUNK

    def key_pos(off):
        return off + lax.broadcasted_iota(I32, (tk, QB), 0)

    for h in range(B_HEADS):
        ql = _dot(q_ref[0, h], wuk_ref[h]) * (B_SCALE * LOG2_E)
        qlat_scr[h * QB:(h + 1) * QB, :] = ql.astype(BF16)

    qi = qi_ref[0].reshape(B_IDX_HEADS * QB, B_IDX_DIM)
    wrow = wrow_ref[0]

    def score_body(kt, carry):
        off = pl.multiple_of(kt * tk, tk)
        rel = jnp.maximum(_dot_nt(kidx_ref[pl.ds(off, tk), :], qi), 0.0) * wrow
        sc = rel[:, 0:QB]
        for h in range(1, B_IDX_HEADS):
            sc = sc + rel[:, h * QB:(h + 1) * QB]
        sc = jnp.where(sc == 0.0, 0.0, sc)
        bits = pltpu.bitcast(sc, I32)
        key = bits ^ ((bits >> 31) & INT_MAX)
        key_scr[pl.ds(off, tk), :] = jnp.where(key_pos(off) < limit, key, INT_MIN)
        return carry

    lax.fori_loop(0, n_kt, score_body, 0)

    def count_keys(pred):
        def body(kt, acc):
            off = pl.multiple_of(kt * tk, tk)
            hit = jnp.where(pred(key_scr[pl.ds(off, tk), :], off), 1.0, 0.0)
            return acc + jnp.sum(hit.reshape(tk // (8 * COUNT_WAYS), COUNT_WAYS, 8, QB), axis=0)
        acc = lax.fori_loop(0, n_kt, body, jnp.zeros((COUNT_WAYS, 8, QB), F32))
        return jnp.sum(jnp.sum(acc, axis=0), axis=0, keepdims=True)

    def bit_body(b, thr_u):
        cand = thr_u | jnp.left_shift(jnp.int32(1), 31 - b)
        cand_s = cand ^ INT_MIN
        cnt = count_keys(lambda key, off: key >= cand_s)
        return jnp.where(cnt >= topk, cand, thr_u)

    thr = lax.fori_loop(0, 32, bit_body, jnp.zeros((1, QB), I32)) ^ INT_MIN
    cnt_gt = count_keys(lambda key, off: key > thr)
    cnt_ge = count_keys(lambda key, off: key >= thr)
    need = topk - cnt_gt

    j0_scr[...] = jnp.full((1, QB), INT_MAX, I32)
    excess = jnp.max(jnp.where(thr > INT_MIN, cnt_ge, 0.0)) > topk

    @pl.when(excess)
    def _():
        def tie_body(b, j0):
            cand = j0 | jnp.left_shift(jnp.int32(1), idx_bits - 1 - b)
            cnt = count_keys(lambda key, off: (key == thr) & (key_pos(off) < cand))
            return jnp.where(cnt < need, cand, j0)
        j0_scr[...] = lax.fori_loop(0, idx_bits, tie_body, jnp.zeros((1, QB), I32))

    j0 = j0_scr[...]

    m_scr[...] = jnp.full(m_scr.shape, NEG_BIG, F32)
    l_scr[...] = jnp.zeros_like(l_scr)
    acc_scr[...] = jnp.zeros_like(acc_scr)

    def att_tile(off, size):
        key = key_scr[pl.ds(off, size), :]
        kpos = off + lax.broadcasted_iota(I32, (size, QB), 0)
        sel = ((key > thr) | ((key == thr) & (kpos <= j0))) & (key > INT_MIN)
        bias = jnp.where(sel, 0.0, NEG_BIG)
        bias = jnp.concatenate([bias] * ATT_HEADS_PER_STEP, axis=1)
        ckv_t = ckv_ref[pl.ds(off, size), :]
        ckvt_t = ckvt_ref[:, pl.ds(off, size)]
        n_groups = B_HEADS // ATT_HEADS_PER_STEP

        def logits(c):
            return _dot_nt(ckv_t, qlat_scr[c * cw:(c + 1) * cw, :]) + bias

        s_next = logits(0)
        for c in range(n_groups):
            cols = slice(c * cw, (c + 1) * cw)
            s = s_next
            if c + 1 < n_groups:
                s_next = logits(c + 1)
            m_old = m_scr[:, cols]
            m_new = jnp.maximum(m_old, jnp.max(s, axis=0, keepdims=True))
            alpha = jnp.exp2(m_old - m_new)
            p = jnp.exp2(s - m_new)
            l_scr[:, cols] = alpha * l_scr[:, cols] + jnp.sum(p, axis=0, keepdims=True)
            acc_scr[:, cols] = alpha * acc_scr[:, cols] + _dot(ckvt_t, p.astype(BF16))
            m_scr[:, cols] = m_new

    def att_pair(j, carry):
        att_tile(pl.multiple_of(j * (2 * tk), 2 * tk), 2 * tk)
        return carry

    lax.fori_loop(0, n_kt // 2, att_pair, 0)

    @pl.when(n_kt % 2 == 1)
    def _():
        att_tile(pl.multiple_of((n_kt - 1) * tk, tk), tk)

    o_lat_t = (acc_scr[...] / l_scr[...]).astype(BF16)
    for h in range(B_HEADS):
        oh = _dot_tn(o_lat_t[:, h * QB:(h + 1) * QB], wuv_ref[h])
        o_ref[:, h * B_HEAD_DIM:(h + 1) * B_HEAD_DIM] = oh.astype(BF16)


def _dsa_attn(q, qi, wrow, kidx, ckv, ckvt, w_uk, w_uv):
    nb = q.shape[0]
    t = kidx.shape[0]
    tk = 512
    topk = min(TOPK_MAX, t // 4)
    idx_bits = max(1, (t - 1).bit_length())
    const2 = lambda i: (0, 0)
    const3 = lambda i: (0, 0, 0)
    hq = B_HEADS * Q_BLOCK
    once = pl.Buffered(1)
    return pl.pallas_call(
        functools.partial(_dsa_attn_kernel, topk, tk, idx_bits),
        grid=(nb,),
        in_specs=[
            pl.BlockSpec((1, B_HEADS, Q_BLOCK, B_HEAD_DIM), lambda i: (i, 0, 0, 0)),
            pl.BlockSpec((1, B_IDX_HEADS, Q_BLOCK, B_IDX_DIM), lambda i: (i, 0, 0, 0)),
            pl.BlockSpec((1, 1, B_IDX_HEADS * Q_BLOCK), lambda i: (i, 0, 0)),
            pl.BlockSpec((t, B_IDX_DIM), const2, pipeline_mode=once),
            pl.BlockSpec((t, B_KV_RANK), const2, pipeline_mode=once),
            pl.BlockSpec((B_KV_RANK, t), const2, pipeline_mode=once),
            pl.BlockSpec((B_HEADS, B_HEAD_DIM, B_KV_RANK), const3, pipeline_mode=once),
            pl.BlockSpec((B_HEADS, B_KV_RANK, B_HEAD_DIM), const3, pipeline_mode=once),
        ],
        out_specs=pl.BlockSpec((Q_BLOCK, B_Q_W), lambda i: (i, 0)),
        out_shape=jax.ShapeDtypeStruct((t, B_Q_W), BF16),
        scratch_shapes=[
            pltpu.VMEM((t, Q_BLOCK), I32),
            pltpu.VMEM((hq, B_KV_RANK), BF16),
            pltpu.VMEM((B_KV_RANK, hq), F32),
            pltpu.VMEM((1, hq), F32),
            pltpu.VMEM((1, hq), F32),
            pltpu.VMEM((1, Q_BLOCK), I32),
        ],
        compiler_params=_cparams(("arbitrary",)),
        name="dsa_attn",
    )(q, qi, wrow, kidx, ckv, ckvt, w_uk, w_uv)


def _final_norm_kernel(x_ref, w_ref, o_ref):
    x = x_ref[...]
    o_ref[...] = x * lax.rsqrt(jnp.mean(x * x, axis=-1, keepdims=True) + RMS_EPS) * w_ref[...]


def _final_norm(x, w):
    t, d = x.shape
    tm = 512
    return pl.pallas_call(
        _final_norm_kernel,
        grid=(t // tm,),
        in_specs=[pl.BlockSpec((tm, d), lambda i: (i, 0)), pl.BlockSpec((1, d), lambda i: (0, 0))],
        out_specs=pl.BlockSpec((tm, d), lambda i: (i, 0)),
        out_shape=jax.ShapeDtypeStruct((t, d), F32),
        compiler_params=_cparams(("arbitrary",)),
        name="final_norm",
    )(x, w)


def _row(v):
    return v.reshape(1, -1)


def _rwkv_layer(x, shift, scale, gate, rkv_first, mu, w_rkv, w0, w1, w2, a0, a1, a2,
                g1, g2, k_k, k_a, r_k, ln_w, ln_b, w_o, vres):
    ups = [_pad_rows(w2), _pad_rows(a2), _pad_rows(g2)]
    if vres is None:
        v0, v1 = None, None
    else:
        v0, v1, v2 = vres
        v0, v1 = _row(v0), _pad_cols(v1)
        ups.append(_pad_rows(v2))
    outs = _rwkv_front(x, shift, scale, mu, w_rkv.astype(BF16), _pad_cols(w1), _pad_cols(a1),
                       _pad_cols(g1), v1)
    rkv, zs = outs[0], outs[1:]
    if vres is None:
        rkv_first = rkv
    y = _rwkv_chunks(rkv, zs, ups, rkv_first, _row(w0), _row(a0), _row(k_k), _row(k_a),
                     _row(r_k), _row(ln_w), _row(ln_b), v0)
    return _proj_res(y, w_o.astype(BF16), x, gate), rkv_first


def _dsa_layer(x, shift, scale, gate, w_in, kv_norm, kidx_norm, w_uk, w_uv, w_o):
    t = x.shape[0]
    n_in = w_in.shape[1]
    n_pad = -(-(n_in - B_IDX_HEADS + LANES) // 768) * 768
    w_in_p = jnp.pad(w_in, ((0, 0), (0, n_pad - n_in))).astype(BF16)
    proj = _norm_proj(x, shift, scale, w_in_p, 768)
    q, qi, ckv, kidx, wi = _dsa_split(proj, _row(kv_norm), _row(kidx_norm))
    nb = t // Q_BLOCK
    wrow = wi[:, :B_IDX_HEADS].reshape(nb, Q_BLOCK, B_IDX_HEADS).transpose(0, 2, 1)
    wrow = wrow.reshape(nb, 1, B_IDX_HEADS * Q_BLOCK)
    o = _dsa_attn(q, qi, wrow, kidx, ckv, ckv.T, w_uk.astype(BF16), w_uv.astype(BF16))
    return _proj_res(o, w_o.astype(BF16), x, gate)


def kernel(x, c, ada_w, ada_b, a_mu, a_w_rkv, a_w0, a_w1, a_w2, a_a0, a_a1, a_a2, a_v0, a_v1, a_v2, a_g1, a_g2, a_k_k, a_k_a, a_r_k, a_ln_w, a_ln_b, a_w_o, b_w_in, b_kv_norm, b_kidx_norm, b_w_uk, b_w_uv, b_w_o, f_w1, f_w3, f_w2, final_norm):
    b, t, d = x.shape
    assert b == 1 and d == D_MODEL
    mod_all = _ada(c, ada_w, ada_b)
    xs = x.reshape(t, d)
    v_first = None
    for i in range(DEPTH):
        shift1, scale1, gate1, shift2, scale2, gate2 = (
            mod_all[i, :, s * d:(s + 1) * d] for s in range(6))
        j = i // 2
        if i % 2 == 0:
            vres = None if j == 0 else (a_v0[j - 1], a_v1[j - 1], a_v2[j - 1])
            xs, v_first = _rwkv_layer(
                xs, shift1, scale1, gate1, v_first, a_mu[j], a_w_rkv[j], a_w0[j], a_w1[j], a_w2[j],
                a_a0[j], a_a1[j], a_a2[j], a_g1[j], a_g2[j], a_k_k[j], a_k_a[j], a_r_k[j],
                a_ln_w[j], a_ln_b[j], a_w_o[j], vres)
        else:
            xs = _dsa_layer(xs, shift1, scale1, gate1, b_w_in[j], b_kv_norm[j], b_kidx_norm[j],
                            b_w_uk[j], b_w_uv[j], b_w_o[j])
        xs = _ffn(xs, shift2, scale2, gate2, f_w1[i].astype(BF16), f_w3[i].astype(BF16),
                  f_w2[i].astype(BF16))
    return _final_norm(xs, _row(final_norm)).reshape(b, t, d)
```

```python
import functools

import jax
import jax.numpy as jnp
from jax import lax
from jax.experimental import pallas as pl
from jax.experimental.pallas import tpu as pltpu

F32 = jnp.float32
BF16 = jnp.bfloat16
I32 = jnp.int32

D_MODEL = 2048
DEPTH = 4
CHUNK = 64
RMS_EPS = 1e-6
A_HEAD = 64
A_HEADS = D_MODEL // A_HEAD
A_GN_EPS = A_HEAD * 1e-5
B_HEADS = 16
B_HEAD_DIM = 128
B_KV_RANK = 512
B_IDX_HEADS = 8
B_IDX_DIM = 128
TOPK_MAX = 256
Q_BLOCK = 128
B_Q_W = B_HEADS * B_HEAD_DIM
B_QI_W = B_IDX_HEADS * B_IDX_DIM
B_SCALE = B_HEAD_DIM ** -0.5
B_IDX_W_SCALE = (B_IDX_HEADS ** -0.5) * (B_IDX_DIM ** -0.5)
D_FF = 5632

LANES = 128
VMEM_LIMIT = 56 * 1024 * 1024
INT_MIN = -(2 ** 31)
INT_MAX = 2 ** 31 - 1
NEG_BIG = -1e30
LOG2_E = 1.4426950408889634
COUNT_WAYS = 8
ATT_HEADS_PER_STEP = 4
RWKV_LANES = 256

_NT = (((1,), (1,)), ((), ()))
_TN = (((0,), (0,)), ((), ()))


def _cparams(sem):
    return pltpu.CompilerParams(dimension_semantics=sem, vmem_limit_bytes=VMEM_LIMIT)


def _dot(a, b):
    return jnp.dot(a, b, preferred_element_type=F32)


def _dot_nt(a, b):
    return lax.dot_general(a, b, _NT, preferred_element_type=F32)


def _dot_tn(a, b):
    return lax.dot_general(a, b, _TN, preferred_element_type=F32)


def _modulate(x, shift, scale):
    ms = jnp.mean(x * x, axis=-1, keepdims=True)
    return x * lax.rsqrt(ms + RMS_EPS) * (1.0 + scale) + shift


def _sigmoid(x):
    return 1.0 / (1.0 + jnp.exp(-x))


def _ada_kernel(c_ref, w_ref, b_ref, o_ref):
    c = c_ref[...]
    s = c * _sigmoid(c)
    o_ref[0] = jnp.sum(s * w_ref[0], axis=0, keepdims=True) + b_ref[0]


def _ada(c, ada_w, ada_b):
    depth, d, n = ada_w.shape
    tn = 1024
    return pl.pallas_call(
        _ada_kernel,
        grid=(depth, n // tn),
        in_specs=[
            pl.BlockSpec((d, 1), lambda l, j: (0, 0)),
            pl.BlockSpec((1, d, tn), lambda l, j: (l, 0, j)),
            pl.BlockSpec((1, 1, tn), lambda l, j: (l, 0, j)),
        ],
        out_specs=pl.BlockSpec((1, 1, tn), lambda l, j: (l, 0, j)),
        out_shape=jax.ShapeDtypeStruct((depth, 1, n), F32),
        compiler_params=_cparams(("arbitrary", "arbitrary")),
        name="ada_mod",
    )(c.reshape(d, 1), ada_w, ada_b.reshape(depth, 1, n))


def _ffn_kernel(x_ref, sh_ref, sc_ref, g_ref, w1_ref, w3_ref, w2_ref, o_ref, h_scr, acc_scr):
    f = pl.program_id(1)

    @pl.when(f == 0)
    def _():
        h_scr[...] = _modulate(x_ref[...], sh_ref[...], sc_ref[...]).astype(BF16)
        acc_scr[...] = jnp.zeros_like(acc_scr)

    h = h_scr[...]
    a = _dot(h, w1_ref[...])
    b = _dot(h, w3_ref[...])
    u = (a * _sigmoid(a) * b).astype(BF16)
    acc_scr[...] += _dot(u, w2_ref[...])

    @pl.when(f == pl.num_programs(1) - 1)
    def _():
        o_ref[...] = x_ref[...] + g_ref[...] * acc_scr[...]


def _ffn(x, shift, scale, gate, w1, w3, w2):
    t, d = x.shape
    dff = w1.shape[1]
    tm, tf = 512, 512
    row = lambda i, f: (0, 0)
    return pl.pallas_call(
        _ffn_kernel,
        grid=(t // tm, dff // tf),
        in_specs=[
            pl.BlockSpec((tm, d), lambda i, f: (i, 0)),
            pl.BlockSpec((1, d), row),
            pl.BlockSpec((1, d), row),
            pl.BlockSpec((1, d), row),
            pl.BlockSpec((d, tf), lambda i, f: (0, f)),
            pl.BlockSpec((d, tf), lambda i, f: (0, f)),
            pl.BlockSpec((tf, d), lambda i, f: (f, 0)),
        ],
        out_specs=pl.BlockSpec((tm, d), lambda i, f: (i, 0)),
        out_shape=jax.ShapeDtypeStruct((t, d), F32),
        scratch_shapes=[pltpu.VMEM((tm, d), BF16), pltpu.VMEM((tm, d), F32)],
        compiler_params=_cparams(("arbitrary", "arbitrary")),
        name="ffn",
    )(x, shift, scale, gate, w1, w3, w2)


def _proj_res_kernel(a_ref, w_ref, x_ref, g_ref, o_ref):
    o_ref[...] = x_ref[...] + g_ref[...] * _dot(a_ref[...], w_ref[...])


def _proj_res(a, w, x, gate):
    t, k = a.shape
    n = w.shape[1]
    tm, tn = 1024, 512
    return pl.pallas_call(
        _proj_res_kernel,
        grid=(t // tm, n // tn),
        in_specs=[
            pl.BlockSpec((tm, k), lambda i, j: (i, 0)),
            pl.BlockSpec((k, tn), lambda i, j: (0, j)),
            pl.BlockSpec((tm, tn), lambda i, j: (i, j)),
            pl.BlockSpec((1, tn), lambda i, j: (0, j)),
        ],
        out_specs=pl.BlockSpec((tm, tn), lambda i, j: (i, j)),
        out_shape=jax.ShapeDtypeStruct((t, n), F32),
        compiler_params=_cparams(("arbitrary", "arbitrary")),
        name="proj_res",
    )(a, w, x, gate)


DSA_TN = 512
DSA_HEADS_PER_TILE = DSA_TN // B_HEAD_DIM
DSA_Q_TILES = B_Q_W // DSA_TN
DSA_KV_TILE = DSA_Q_TILES
DSA_QI_TILE0 = DSA_KV_TILE + B_KV_RANK // DSA_TN
DSA_QI_TILES = B_QI_W // DSA_TN
DSA_KI_TILE = DSA_QI_TILE0 + DSA_QI_TILES


def _dsa_front_kernel(x_ref, sh_ref, sc_ref, w_ref, kvn_ref, kin_ref,
                      q_ref, qi_ref, ckv_ref, ckvt_ref, kidx_ref, wrow_ref, h_scr):
    j = pl.program_id(1)
    nbk = x_ref.shape[0] // Q_BLOCK

    @pl.when(j == 0)
    def _():
        h_scr[...] = _modulate(x_ref[...], sh_ref[...], sc_ref[...]).astype(BF16)

    y = _dot(h_scr[...], w_ref[...])

    def put_heads(ref):
        for b in range(nbk):
            for h in range(DSA_HEADS_PER_TILE):
                ref[b, h] = y[b * Q_BLOCK:(b + 1) * Q_BLOCK,
                              h * B_HEAD_DIM:(h + 1) * B_HEAD_DIM].astype(BF16)

    @pl.when(j < DSA_Q_TILES)
    def _():
        put_heads(q_ref)

    @pl.when(j == DSA_KV_TILE)
    def _():
        ckv = y * lax.rsqrt(jnp.mean(y * y, axis=-1, keepdims=True) + RMS_EPS) * kvn_ref[...]
        ckv_ref[...] = ckv.astype(BF16)
        ckvt_ref[...] = ckv.T.astype(BF16)

    @pl.when((j >= DSA_QI_TILE0) & (j < DSA_KI_TILE))
    def _():
        put_heads(qi_ref)

    @pl.when(j == DSA_KI_TILE)
    def _():
        kidx = y[:, :B_IDX_DIM]
        kidx = kidx * lax.rsqrt(jnp.mean(kidx * kidx, axis=-1, keepdims=True) + RMS_EPS) * kin_ref[...]
        kidx_ref[...] = kidx.astype(BF16)
        wi = y[:, B_IDX_DIM:B_IDX_DIM + LANES] * B_IDX_W_SCALE
        for b in range(nbk):
            wrow_ref[b] = wi[b * Q_BLOCK:(b + 1) * Q_BLOCK, :].T[:B_IDX_HEADS, :]


def _dsa_front(x, shift, scale, w_in, kv_norm, kidx_norm):
    t, d = x.shape
    assert B_HEAD_DIM == B_IDX_DIM and B_KV_RANK == DSA_TN and w_in.shape[1] <= (DSA_KI_TILE + 1) * DSA_TN
    n_tiles = DSA_KI_TILE + 1
    w = jnp.pad(w_in, ((0, 0), (0, n_tiles * DSA_TN - w_in.shape[1]))).astype(BF16)
    tm = 512
    nbk = tm // Q_BLOCK
    nb = t // Q_BLOCK
    row = lambda i, j: (0, 0)
    hpt = DSA_HEADS_PER_TILE
    return pl.pallas_call(
        _dsa_front_kernel,
        grid=(t // tm, n_tiles),
        in_specs=[
            pl.BlockSpec((tm, d), lambda i, j: (i, 0)),
            pl.BlockSpec((1, d), row),
            pl.BlockSpec((1, d), row),
            pl.BlockSpec((d, DSA_TN), lambda i, j: (0, j)),
            pl.BlockSpec((1, B_KV_RANK), row),
            pl.BlockSpec((1, B_IDX_DIM), row),
        ],
        out_specs=[
            pl.BlockSpec((nbk, hpt, Q_BLOCK, B_HEAD_DIM),
                         lambda i, j: (i, jnp.minimum(j, DSA_Q_TILES - 1), 0, 0)),
            pl.BlockSpec((nbk, hpt, Q_BLOCK, B_IDX_DIM),
                         lambda i, j: (i, jnp.clip(j - DSA_QI_TILE0, 0, DSA_QI_TILES - 1), 0, 0)),
            pl.BlockSpec((tm, B_KV_RANK), lambda i, j: (i, 0)),
            pl.BlockSpec((B_KV_RANK, tm), lambda i, j: (0, i)),
            pl.BlockSpec((tm, B_IDX_DIM), lambda i, j: (i, 0)),
            pl.BlockSpec((nbk, B_IDX_HEADS, Q_BLOCK), lambda i, j: (i, 0, 0)),
        ],
        out_shape=[
            jax.ShapeDtypeStruct((nb, B_HEADS, Q_BLOCK, B_HEAD_DIM), BF16),
            jax.ShapeDtypeStruct((nb, B_IDX_HEADS, Q_BLOCK, B_IDX_DIM), BF16),
            jax.ShapeDtypeStruct((t, B_KV_RANK), BF16),
            jax.ShapeDtypeStruct((B_KV_RANK, t), BF16),
            jax.ShapeDtypeStruct((t, B_IDX_DIM), BF16),
            jax.ShapeDtypeStruct((nb, B_IDX_HEADS, Q_BLOCK), F32),
        ],
        scratch_shapes=[pltpu.VMEM((tm, d), BF16)],
        compiler_params=_cparams(("arbitrary", "arbitrary")),
        name="dsa_front",
    )(x, shift, scale, w, kv_norm, kidx_norm)


def _rwkv_front_kernel(has_vres, x_ref, sh_ref, sc_ref, mu_ref, wrkv_ref, w1_ref, a1_ref, g1_ref,
                       *rest):
    if has_vres:
        v1_ref, rkv_ref, zw_ref, za_ref, zg_ref, zv_ref, carry = rest
    else:
        rkv_ref, zw_ref, za_ref, zg_ref, carry = rest
    i = pl.program_id(0)
    tm = x_ref.shape[0]

    @pl.when(i == 0)
    def _():
        carry[...] = jnp.zeros_like(carry)

    h = _modulate(x_ref[...], sh_ref[...], sc_ref[...])
    prev_last = carry[7:8, :]
    rolled = pltpu.roll(h, 1, axis=0)
    rows = lax.broadcasted_iota(I32, h.shape, 0)
    h_prev = jnp.where(rows == 0, prev_last, rolled)
    carry[...] = h[tm - 8:tm, :]
    dx = h_prev - h

    def mix(j):
        return (h + dx * mu_ref[j:j + 1, :]).astype(BF16)

    xv = mix(3)
    rkv_ref[0] = _dot(mix(0), wrkv_ref[0])
    rkv_ref[1] = _dot(mix(2), wrkv_ref[1])
    rkv_ref[2] = _dot(xv, wrkv_ref[2])
    zw_ref[...] = jnp.tanh(_dot(mix(1), w1_ref[...])).astype(BF16)
    za_ref[...] = _dot(mix(4), a1_ref[...]).astype(BF16)
    zg_ref[...] = _sigmoid(_dot(mix(5), g1_ref[...])).astype(BF16)
    if has_vres:
        zv_ref[...] = _dot(xv, v1_ref[...]).astype(BF16)


def _pad_cols(w):
    r = w.shape[1]
    return jnp.pad(w, ((0, 0), (0, -(-r // LANES) * LANES - r))).astype(BF16)


def _pad_rows(w):
    r = w.shape[0]
    return jnp.pad(w, ((0, -(-r // LANES) * LANES - r), (0, 0))).astype(BF16)


def _rwkv_front(x, shift, scale, mu, w_rkv, w1, a1, g1, v1):
    t, d = x.shape
    tm = 256
    has_vres = v1 is not None
    once = pl.Buffered(1)
    lora_ws = [w1, a1, g1] + ([v1] if has_vres else [])
    const2 = lambda i: (0, 0)
    in_specs = [
        pl.BlockSpec((tm, d), lambda i: (i, 0)),
        pl.BlockSpec((1, d), const2),
        pl.BlockSpec((1, d), const2),
        pl.BlockSpec((6, d), const2),
        pl.BlockSpec((3, d, d), lambda i: (0, 0, 0), pipeline_mode=once),
    ] + [pl.BlockSpec(w.shape, const2, pipeline_mode=once) for w in lora_ws]
    out_specs = [pl.BlockSpec((3, tm, d), lambda i: (0, i, 0))]
    out_specs += [pl.BlockSpec((tm, w.shape[1]), lambda i: (i, 0)) for w in lora_ws]
    out_shape = [jax.ShapeDtypeStruct((3, t, d), F32)]
    out_shape += [jax.ShapeDtypeStruct((t, w.shape[1]), BF16) for w in lora_ws]
    return pl.pallas_call(
        functools.partial(_rwkv_front_kernel, has_vres),
        grid=(t // tm,),
        in_specs=in_specs,
        out_specs=out_specs,
        out_shape=out_shape,
        scratch_shapes=[pltpu.VMEM((8, d), F32)],
        compiler_params=_cparams(("arbitrary",)),
        name="rwkv_front",
    )(x, shift, scale, mu, w_rkv, *lora_ws)


def _split2(x):
    hi = x.astype(BF16)
    lo = (x - hi.astype(F32)).astype(BF16)
    return hi, lo


def _bdot(a, b):
    return lax.dot_general(a, b, (((2,), (1,)), ((0,), (0,))), preferred_element_type=F32)


def _bdot_nt(a, b):
    return lax.dot_general(a, b, (((2,), (2,)), ((0,), (0,))), preferred_element_type=F32)


def _bdot_tn(a, b):
    return lax.dot_general(a, b, (((1,), (1,)), ((0,), (0,))), preferred_element_type=F32)


def _rwkv_chunk_kernel(has_vres, n_chunks, *refs):
    if has_vres:
        (r_ref, k_ref, v_ref, zw_ref, za_ref, zg_ref, zv_ref, vf_ref,
         w2_ref, a2_ref, g2_ref, v2_ref,
         w0_ref, a0_ref, kk_ref, ka_ref, rk_ref, lnw_ref, lnb_ref, v0_ref,
         o_ref, s_scr) = refs
    else:
        (r_ref, k_ref, v_ref, zw_ref, za_ref, zg_ref,
         w2_ref, a2_ref, g2_ref,
         w0_ref, a0_ref, kk_ref, ka_ref, rk_ref, lnw_ref, lnb_ref,
         o_ref, s_scr) = refs

    L = CHUNK
    W = o_ref.shape[1]
    NH = W // A_HEAD
    G = n_chunks
    R = G * L

    @pl.when(pl.program_id(1) == 0)
    def _():
        s_scr[...] = jnp.zeros_like(s_scr)

    lane = lax.broadcasted_iota(I32, (1, W), 1)
    head_masks = [((lane // A_HEAD) == e).astype(F32) for e in range(NH)]
    rr = lax.broadcasted_iota(I32, (L, L), 0)
    cc = lax.broadcasted_iota(I32, (L, L), 1)
    rr2 = lax.broadcasted_iota(I32, (L, 2 * L), 0)
    cc2 = lax.broadcasted_iota(I32, (L, 2 * L), 1) % L
    low_strict2 = (rr2 > cc2)[None]
    low_incl2 = (rr2 >= cc2)[None]
    eye_l = (rr == cc).astype(F32)[None]
    diag_blk = ((rr >> 3) == (cc >> 3))[None]
    merge_blks = [(((rr >> (s + 1)) == (cc >> (s + 1))) & ((rr >> s) > (cc >> s)))[None]
                  for s in (3, 4, 5)]
    r2 = lax.broadcasted_iota(I32, (W, W), 0)
    c2 = lax.broadcasted_iota(I32, (W, W), 1)
    same_head = (r2 // A_HEAD) == (c2 // A_HEAD)
    seg_ones = same_head.astype(BF16)
    r3 = lax.broadcasted_iota(I32, (R, R), 0)
    c3 = lax.broadcasted_iota(I32, (R, R), 1)
    tri_ones = (((r3 // L) == (c3 // L)) & (r3 >= c3)).astype(BF16)

    def segsum(x):
        hi, lo = _split2(x)
        return _dot(hi, seg_ones) + _dot(lo, seg_ones)

    r = r_ref[0]
    k = k_ref[0]
    v = v_ref[0]
    z = -(w0_ref[...] + _dot(zw_ref[...], w2_ref[...]))
    softplus = jnp.maximum(z, 0.0) + jnp.log(1.0 + jnp.exp(-jnp.abs(z)))
    logw = -jnp.exp(-softplus - 0.5)
    a = _sigmoid(a0_ref[...] + _dot(za_ref[...], a2_ref[...]))
    if has_vres:
        v = v + (vf_ref[0] - v) * _sigmoid(v0_ref[...] + _dot(zv_ref[...], v2_ref[...]))
    kk = k * kk_ref[...]
    kk = kk / jnp.maximum(jnp.sqrt(segsum(kk * kk)), 1e-12)
    k = k * (1.0 + (a - 1.0) * ka_ref[...])

    h1, h2 = _split2(logw)
    cum = _dot(tri_ones, h1) + _dot(tri_ones, h2)
    w_in = jnp.exp(cum)
    w_inv = jnp.exp(-cum)
    a_t = -kk * jnp.exp(cum - logw)
    b_t = kk * a * w_inv
    k_t = k * w_inv
    r_t = r * w_in

    def c3d(x):
        return x.reshape(G, L, W)

    def per_head(x):
        x3 = c3d(x)
        return jnp.concatenate([x3 * hm for hm in head_masks], axis=0).astype(BF16)

    def head_sum(x):
        out = x[:G]
        for e in range(1, NH):
            out = out + x[e * G:(e + 1) * G]
        return out

    a_e = per_head(a_t)
    r_e = per_head(r_t)
    v_e = per_head(v)
    bk_3 = jnp.concatenate([c3d(b_t), c3d(k_t)], axis=1).astype(BF16)
    bk_e = jnp.concatenate([bk_3] * NH, axis=0)
    zeros_e = jnp.zeros_like(v_e)

    x_a = jnp.where(low_strict2, _bdot_nt(a_e, bk_e), 0.0)
    x_r = jnp.where(low_incl2, _bdot_nt(r_e, bk_e), 0.0).astype(BF16)
    x_ab = x_a[:, :, :L]

    xp = jnp.where(diag_blk, x_ab, 0.0)
    tm = eye_l + xp
    for _ in range(2):
        xb = xp.astype(BF16)
        xp = _bdot(xb, xb)
        tm = tm + _bdot(tm.astype(BF16), xp.astype(BF16))
    for off_blk in merge_blks:
        tb = tm.astype(BF16)
        x_off = jnp.where(off_blk, x_ab, 0.0).astype(BF16)
        tm = tm + _bdot(_bdot(tb, x_off).astype(BF16), tb)
    tb = tm.astype(BF16)

    p_e = _bdot(tb, a_e)
    tx = _bdot(tb, x_a.astype(BF16)).astype(BF16)
    q_e = _bdot(tx, jnp.concatenate([zeros_e, v_e], axis=1))
    p_eb = p_e.astype(BF16)
    q_eb = q_e.astype(BF16)
    rp_e = _bdot(x_r, jnp.concatenate([p_eb, zeros_e], axis=1))
    y0_e = _bdot(x_r, jnp.concatenate([q_eb, v_e], axis=1))
    p_sum = head_sum(p_e)
    q_sum = head_sum(q_e)
    rp = (c3d(r_t) + head_sum(rp_e)).astype(BF16)
    y0 = head_sum(y0_e)

    b_3 = bk_3[:, :L, :]
    w_last = c3d(w_in)[:, L - 1:L, :]
    m_lr = (jnp.where(same_head[None], _bdot_tn(p_sum.astype(BF16), b_3), 0.0) * w_last).astype(BF16)
    qv = jnp.concatenate([q_sum, c3d(v)], axis=1).astype(BF16)
    c_bd = jnp.where(same_head[None], _bdot_tn(qv, bk_3), 0.0) * w_last

    s = s_scr[...]
    ys = []
    for c in range(G):
        s_b = s.astype(BF16)
        ys.append(_dot_nt(rp[c], s_b) + y0[c])
        s = s * w_last[c] + _dot(s_b, m_lr[c]) + c_bd[c]
    s_scr[...] = s
    y = jnp.concatenate(ys, axis=0)

    mean = segsum(y) * (1.0 / A_HEAD)
    dlt = y - mean
    var = segsum(dlt * dlt) * (1.0 / A_HEAD)
    yn = dlt * lax.rsqrt(var + A_GN_EPS) * lnw_ref[...] + lnb_ref[...]
    bonus = segsum(r * k * rk_ref[...]) * v
    o_ref[...] = ((yn + bonus) * _dot(zg_ref[...], g2_ref[...])).astype(BF16)


def _rwkv_chunks(rkv, zs, ups, rkv_first, w0, a0, k_k, k_a, r_k, ln_w, ln_b, v0):
    _, t, d = rkv.shape
    n_chunks = 8
    tm = n_chunks * CHUNK
    has_vres = len(zs) == 4
    blk = lambda p, c: (c, p)
    prm = lambda p, c: (0, p)
    w = RWKV_LANES
    slab = lambda j: pl.BlockSpec((1, tm, w), lambda p, c: (j, c, p))
    in_specs = [slab(0), slab(1), slab(2)]
    in_specs += [pl.BlockSpec((tm, z.shape[1]), lambda p, c: (c, 0)) for z in zs]
    if has_vres:
        in_specs += [slab(2)]
    in_specs += [pl.BlockSpec((u.shape[0], w), prm) for u in ups]
    in_specs += [pl.BlockSpec((1, w), prm)] * (8 if has_vres else 7)
    args = [rkv, rkv, rkv] + list(zs)
    if has_vres:
        args += [rkv_first]
    args += list(ups)
    args += [w0, a0, k_k, k_a, r_k, ln_w, ln_b]
    if has_vres:
        args += [v0]
    return pl.pallas_call(
        functools.partial(_rwkv_chunk_kernel, has_vres, n_chunks),
        grid=(d // w, t // tm),
        in_specs=in_specs,
        out_specs=pl.BlockSpec((tm, w), blk),
        out_shape=jax.ShapeDtypeStruct((t, d), BF16),
        scratch_shapes=[pltpu.VMEM((w, w), F32)],
        compiler_params=_cparams(("arbitrary", "arbitrary")),
        name="rwkv_chunks",
    )(*args)


def _dsa_attn_kernel(topk, tk, idx_bits,
                     q_ref, qi_ref, wrow_ref, kidx_ref, ckv_ref, ckvt_ref, wuk_ref, wuv_ref, o_ref,
                     key_scr, qlat_scr, acc_scr, m_scr, l_scr, j0_scr):
    i = pl.program_id(0)
    QB = Q_BLOCK
    cw = ATT_HEADS_PER_STEP * QB
    n_kt = (i * QB + QB + tk - 1) // tk
    qpos = i * QB + lax.broadcasted_iota(I32, (1, QB), 1)
    limit = (qpos // CHUNK + 1) * CHUNK

    def key_pos(off):
        return off + lax.broadcasted_iota(I32, (tk, QB), 0)

    for h in range(B_HEADS):
        ql = _dot(q_ref[0, h], wuk_ref[h]) * (B_SCALE * LOG2_E)
        qlat_scr[h * QB:(h + 1) * QB, :] = ql.astype(BF16)

    qi = qi_ref[0].reshape(B_IDX_HEADS * QB, B_IDX_DIM)
    wrow = wrow_ref[0]

    def score_body(kt, carry):
        off = pl.multiple_of(kt * tk, tk)
        rel = jnp.maximum(_dot_nt(kidx_ref[pl.ds(off, tk), :], qi), 0.0)
        sc = rel[:, 0:QB] * wrow[0:1, :]
        for h in range(1, B_IDX_HEADS):
            sc = sc + rel[:, h * QB:(h + 1) * QB] * wrow[h:h + 1, :]
        sc = jnp.where(sc == 0.0, 0.0, sc)
        bits = pltpu.bitcast(sc, I32)
        key = bits ^ ((bits >> 31) & INT_MAX)
        key_scr[pl.ds(off, tk), :] = jnp.where(key_pos(off) < limit, key, INT_MIN)
        return carry

    lax.fori_loop(0, n_kt, score_body, 0)

    def count_keys(pred):
        def body(kt, acc):
            off = pl.multiple_of(kt * tk, tk)
            hit = jnp.where(pred(key_scr[pl.ds(off, tk), :], off), 1.0, 0.0)
            return acc + jnp.sum(hit.reshape(tk // (8 * COUNT_WAYS), COUNT_WAYS, 8, QB), axis=0)
        acc = lax.fori_loop(0, n_kt, body, jnp.zeros((COUNT_WAYS, 8, QB), F32))
        return jnp.sum(jnp.sum(acc, axis=0), axis=0, keepdims=True)

    def bit_body(b, st):
        thr_u, cnt_ge = st
        cand = thr_u | jnp.left_shift(jnp.int32(1), 31 - b)
        cand_s = cand ^ INT_MIN
        cnt = count_keys(lambda key, off: key >= cand_s)
        take = cnt >= topk
        return jnp.where(take, cand, thr_u), jnp.where(take, cnt, cnt_ge)

    n_scanned = jnp.zeros((1, QB), F32) + (n_kt * tk).astype(F32)
    thr_u, cnt_ge = lax.fori_loop(0, 32, bit_body, (jnp.zeros((1, QB), I32), n_scanned))
    thr = thr_u ^ INT_MIN

    j0_scr[...] = jnp.full((1, QB), INT_MAX, I32)
    excess = jnp.max(jnp.where(thr > INT_MIN, cnt_ge, 0.0)) > topk

    @pl.when(excess)
    def _():
        need = topk - count_keys(lambda key, off: key > thr)

        def tie_body(b, j0):
            cand = j0 | jnp.left_shift(jnp.int32(1), idx_bits - 1 - b)
            cnt = count_keys(lambda key, off: (key == thr) & (key_pos(off) < cand))
            return jnp.where(cnt < need, cand, j0)
        j0_scr[...] = lax.fori_loop(0, idx_bits, tie_body, jnp.zeros((1, QB), I32))

    j0 = j0_scr[...]

    m_scr[...] = jnp.full(m_scr.shape, NEG_BIG, F32)
    l_scr[...] = jnp.zeros_like(l_scr)
    acc_scr[...] = jnp.zeros_like(acc_scr)

    def att_tile(off, size):
        key = key_scr[pl.ds(off, size), :]
        kpos = off + lax.broadcasted_iota(I32, (size, QB), 0)
        sel = ((key > thr) | ((key == thr) & (kpos <= j0))) & (key > INT_MIN)
        bias = jnp.where(sel, 0.0, NEG_BIG)
        bias = jnp.concatenate([bias] * ATT_HEADS_PER_STEP, axis=1)
        ckv_t = ckv_ref[pl.ds(off, size), :]
        ckvt_t = ckvt_ref[:, pl.ds(off, size)]
        n_groups = B_HEADS // ATT_HEADS_PER_STEP

        def logits(c):
            return _dot_nt(ckv_t, qlat_scr[c * cw:(c + 1) * cw, :]) + bias

        s_next = logits(0)
        for c in range(n_groups):
            cols = slice(c * cw, (c + 1) * cw)
            s = s_next
            if c + 1 < n_groups:
                s_next = logits(c + 1)
            m_old = m_scr[:, cols]
            m_new = jnp.maximum(m_old, jnp.max(s, axis=0, keepdims=True))
            alpha = jnp.exp2(m_old - m_new)
            p = jnp.exp2(s - m_new)
            l_scr[:, cols] = alpha * l_scr[:, cols] + jnp.sum(p, axis=0, keepdims=True)
            acc_scr[:, cols] = alpha * acc_scr[:, cols] + _dot(ckvt_t, p.astype(BF16))
            m_scr[:, cols] = m_new

    def att_pair(j, carry):
        att_tile(pl.multiple_of(j * (2 * tk), 2 * tk), 2 * tk)
        return carry

    lax.fori_loop(0, n_kt // 2, att_pair, 0)

    @pl.when(n_kt % 2 == 1)
    def _():
        att_tile(pl.multiple_of((n_kt - 1) * tk, tk), tk)

    o_lat_t = (acc_scr[...] / l_scr[...]).astype(BF16)
    for h in range(B_HEADS):
        oh = _dot_tn(o_lat_t[:, h * QB:(h + 1) * QB], wuv_ref[h])
        o_ref[:, h * B_HEAD_DIM:(h + 1) * B_HEAD_DIM] = oh.astype(BF16)


def _dsa_attn(q, qi, wrow, kidx, ckv, ckvt, w_uk, w_uv):
    nb = q.shape[0]
    t = kidx.shape[0]
    tk = 512
    topk = min(TOPK_MAX, t // 4)
    idx_bits = max(1, (t - 1).bit_length())
    const2 = lambda i: (0, 0)
    const3 = lambda i: (0, 0, 0)
    hq = B_HEADS * Q_BLOCK
    once = pl.Buffered(1)
    return pl.pallas_call(
        functools.partial(_dsa_attn_kernel, topk, tk, idx_bits),
        grid=(nb,),
        in_specs=[
            pl.BlockSpec((1, B_HEADS, Q_BLOCK, B_HEAD_DIM), lambda i: (i, 0, 0, 0)),
            pl.BlockSpec((1, B_IDX_HEADS, Q_BLOCK, B_IDX_DIM), lambda i: (i, 0, 0, 0)),
            pl.BlockSpec((1, B_IDX_HEADS, Q_BLOCK), lambda i: (i, 0, 0)),
            pl.BlockSpec((t, B_IDX_DIM), const2, pipeline_mode=once),
            pl.BlockSpec((t, B_KV_RANK), const2, pipeline_mode=once),
            pl.BlockSpec((B_KV_RANK, t), const2, pipeline_mode=once),
            pl.BlockSpec((B_HEADS, B_HEAD_DIM, B_KV_RANK), const3, pipeline_mode=once),
            pl.BlockSpec((B_HEADS, B_KV_RANK, B_HEAD_DIM), const3, pipeline_mode=once),
        ],
        out_specs=pl.BlockSpec((Q_BLOCK, B_Q_W), lambda i: (i, 0)),
        out_shape=jax.ShapeDtypeStruct((t, B_Q_W), BF16),
        scratch_shapes=[
            pltpu.VMEM((t, Q_BLOCK), I32),
            pltpu.VMEM((hq, B_KV_RANK), BF16),
            pltpu.VMEM((B_KV_RANK, hq), F32),
            pltpu.VMEM((1, hq), F32),
            pltpu.VMEM((1, hq), F32),
            pltpu.VMEM((1, Q_BLOCK), I32),
        ],
        compiler_params=_cparams(("arbitrary",)),
        name="dsa_attn",
    )(q, qi, wrow, kidx, ckv, ckvt, w_uk, w_uv)


def _final_norm_kernel(x_ref, w_ref, o_ref):
    x = x_ref[...]
    o_ref[...] = x * lax.rsqrt(jnp.mean(x * x, axis=-1, keepdims=True) + RMS_EPS) * w_ref[...]


def _final_norm(x, w):
    t, d = x.shape
    tm = 512
    return pl.pallas_call(
        _final_norm_kernel,
        grid=(t // tm,),
        in_specs=[pl.BlockSpec((tm, d), lambda i: (i, 0)), pl.BlockSpec((1, d), lambda i: (0, 0))],
        out_specs=pl.BlockSpec((tm, d), lambda i: (i, 0)),
        out_shape=jax.ShapeDtypeStruct((t, d), F32),
        compiler_params=_cparams(("arbitrary",)),
        name="final_norm",
    )(x, w)


def _row(v):
    return v.reshape(1, -1)


def _rwkv_layer(x, shift, scale, gate, rkv_first, mu, w_rkv, w0, w1, w2, a0, a1, a2,
                g1, g2, k_k, k_a, r_k, ln_w, ln_b, w_o, vres):
    ups = [_pad_rows(w2), _pad_rows(a2), _pad_rows(g2)]
    if vres is None:
        v0, v1 = None, None
    else:
        v0, v1, v2 = vres
        v0, v1 = _row(v0), _pad_cols(v1)
        ups.append(_pad_rows(v2))
    outs = _rwkv_front(x, shift, scale, mu, w_rkv.astype(BF16), _pad_cols(w1), _pad_cols(a1),
                       _pad_cols(g1), v1)
    rkv, zs = outs[0], outs[1:]
    if vres is None:
        rkv_first = rkv
    y = _rwkv_chunks(rkv, zs, ups, rkv_first, _row(w0), _row(a0), _row(k_k), _row(k_a),
                     _row(r_k), _row(ln_w), _row(ln_b), v0)
    return _proj_res(y, w_o.astype(BF16), x, gate), rkv_first


def _dsa_layer(x, shift, scale, gate, w_in, kv_norm, kidx_norm, w_uk, w_uv, w_o):
    q, qi, ckv, ckvt, kidx, wrow = _dsa_front(x, shift, scale, w_in, _row(kv_norm), _row(kidx_norm))
    o = _dsa_attn(q, qi, wrow, kidx, ckv, ckvt, w_uk.astype(BF16), w_uv.astype(BF16))
    return _proj_res(o, w_o.astype(BF16), x, gate)


def kernel(x, c, ada_w, ada_b, a_mu, a_w_rkv, a_w0, a_w1, a_w2, a_a0, a_a1, a_a2, a_v0, a_v1, a_v2, a_g1, a_g2, a_k_k, a_k_a, a_r_k, a_ln_w, a_ln_b, a_w_o, b_w_in, b_kv_norm, b_kidx_norm, b_w_uk, b_w_uv, b_w_o, f_w1, f_w3, f_w2, final_norm):
    b, t, d = x.shape
    assert b == 1 and d == D_MODEL
    mod_all = _ada(c, ada_w, ada_b)
    xs = x.reshape(t, d)
    v_first = None
    for i in range(DEPTH):
        shift1, scale1, gate1, shift2, scale2, gate2 = (
            mod_all[i, :, s * d:(s + 1) * d] for s in range(6))
        j = i // 2
        if i % 2 == 0:
            vres = None if j == 0 else (a_v0[j - 1], a_v1[j - 1], a_v2[j - 1])
            xs, v_first = _rwkv_layer(
                xs, shift1, scale1, gate1, v_first, a_mu[j], a_w_rkv[j], a_w0[j], a_w1[j], a_w2[j],
                a_a0[j], a_a1[j], a_a2[j], a_g1[j], a_g2[j], a_k_k[j], a_k_a[j], a_r_k[j],
                a_ln_w[j], a_ln_b[j], a_w_o[j], vres)
        else:
            xs = _dsa_layer(xs, shift1, scale1, gate1, b_w_in[j], b_kv_norm[j], b_kidx_norm[j],
                            b_w_uk[j], b_w_uv[j], b_w_o[j])
        xs = _ffn(xs, shift2, scale2, gate2, f_w1[i].astype(BF16), f_w3[i].astype(BF16),
                  f_w2[i].astype(BF16))
    return _final_norm(xs, _row(final_norm)).reshape(b, t, d)
```

```python
import functools

import jax
import jax.numpy as jnp
from jax import lax
from jax.experimental import pallas as pl
from jax.experimental.pallas import tpu as pltpu

F32 = jnp.float32
BF16 = jnp.bfloat16
I32 = jnp.int32
I16 = jnp.int16

D_MODEL = 2048
DEPTH = 4
CHUNK = 64
RMS_EPS = 1e-6
A_HEAD = 64
A_HEADS = D_MODEL // A_HEAD
A_GN_EPS = A_HEAD * 1e-5
B_HEADS = 16
B_HEAD_DIM = 128
B_KV_RANK = 512
B_IDX_HEADS = 8
B_IDX_DIM = 128
TOPK_MAX = 256
Q_BLOCK = 128
B_Q_W = B_HEADS * B_HEAD_DIM
B_QI_W = B_IDX_HEADS * B_IDX_DIM
B_SCALE = B_HEAD_DIM ** -0.5
B_IDX_W_SCALE = (B_IDX_HEADS ** -0.5) * (B_IDX_DIM ** -0.5)
D_FF = 5632

LANES = 128
VMEM_LIMIT = 56 * 1024 * 1024
INT_MIN = -(2 ** 31)
INT_MAX = 2 ** 31 - 1
NEG_BIG = -1e30
LOG2_E = 1.4426950408889634
COUNT_WAYS = 8
ATT_HEADS_PER_STEP = 4
RWKV_LANES = 256

_NT = (((1,), (1,)), ((), ()))
_TN = (((0,), (0,)), ((), ()))


def _cparams(sem):
    return pltpu.CompilerParams(dimension_semantics=sem, vmem_limit_bytes=VMEM_LIMIT)


def _dot(a, b):
    return jnp.dot(a, b, preferred_element_type=F32)


def _dot_nt(a, b):
    return lax.dot_general(a, b, _NT, preferred_element_type=F32)


def _dot_tn(a, b):
    return lax.dot_general(a, b, _TN, preferred_element_type=F32)


def _modulate(x, shift, scale):
    ms = jnp.mean(x * x, axis=-1, keepdims=True)
    return x * lax.rsqrt(ms + RMS_EPS) * (1.0 + scale) + shift


def _sigmoid(x):
    return 1.0 / (1.0 + jnp.exp(-x))


def _ada_kernel(c_ref, w_ref, b_ref, o_ref):
    c = c_ref[...]
    s = c * _sigmoid(c)
    o_ref[0] = jnp.sum(s * w_ref[0], axis=0, keepdims=True) + b_ref[0]


def _ada(c, ada_w, ada_b):
    depth, d, n = ada_w.shape
    tn = 1024
    return pl.pallas_call(
        _ada_kernel,
        grid=(depth, n // tn),
        in_specs=[
            pl.BlockSpec((d, 1), lambda l, j: (0, 0)),
            pl.BlockSpec((1, d, tn), lambda l, j: (l, 0, j)),
            pl.BlockSpec((1, 1, tn), lambda l, j: (l, 0, j)),
        ],
        out_specs=pl.BlockSpec((1, 1, tn), lambda l, j: (l, 0, j)),
        out_shape=jax.ShapeDtypeStruct((depth, 1, n), F32),
        compiler_params=_cparams(("arbitrary", "arbitrary")),
        name="ada_mod",
    )(c.reshape(d, 1), ada_w, ada_b.reshape(depth, 1, n))


def _ffn_kernel(x_ref, sh_ref, sc_ref, g_ref, w1_ref, w3_ref, w2_ref, o_ref, h_scr, acc_scr):
    f = pl.program_id(1)

    @pl.when(f == 0)
    def _():
        h_scr[...] = _modulate(x_ref[...], sh_ref[...], sc_ref[...]).astype(BF16)
        acc_scr[...] = jnp.zeros_like(acc_scr)

    h = h_scr[...]
    a = _dot(h, w1_ref[...])
    b = _dot(h, w3_ref[...])
    u = (a * _sigmoid(a) * b).astype(BF16)
    acc_scr[...] += _dot(u, w2_ref[...])

    @pl.when(f == pl.num_programs(1) - 1)
    def _():
        o_ref[...] = x_ref[...] + g_ref[...] * acc_scr[...]


def _ffn(x, shift, scale, gate, w1, w3, w2):
    t, d = x.shape
    dff = w1.shape[1]
    tm, tf = 512, 512
    row = lambda i, f: (0, 0)
    return pl.pallas_call(
        _ffn_kernel,
        grid=(t // tm, dff // tf),
        in_specs=[
            pl.BlockSpec((tm, d), lambda i, f: (i, 0)),
            pl.BlockSpec((1, d), row),
            pl.BlockSpec((1, d), row),
            pl.BlockSpec((1, d), row),
            pl.BlockSpec((d, tf), lambda i, f: (0, f)),
            pl.BlockSpec((d, tf), lambda i, f: (0, f)),
            pl.BlockSpec((tf, d), lambda i, f: (f, 0)),
        ],
        out_specs=pl.BlockSpec((tm, d), lambda i, f: (i, 0)),
        out_shape=jax.ShapeDtypeStruct((t, d), F32),
        scratch_shapes=[pltpu.VMEM((tm, d), BF16), pltpu.VMEM((tm, d), F32)],
        compiler_params=_cparams(("arbitrary", "arbitrary")),
        name="ffn",
    )(x, shift, scale, gate, w1, w3, w2)


def _proj_res_kernel(a_ref, w_ref, x_ref, g_ref, o_ref):
    o_ref[...] = x_ref[...] + g_ref[...] * _dot(a_ref[...], w_ref[...])


def _proj_res(a, w, x, gate):
    t, k = a.shape
    n = w.shape[1]
    tm, tn = 1024, 512
    return pl.pallas_call(
        _proj_res_kernel,
        grid=(t // tm, n // tn),
        in_specs=[
            pl.BlockSpec((tm, k), lambda i, j: (i, 0)),
            pl.BlockSpec((k, tn), lambda i, j: (0, j)),
            pl.BlockSpec((tm, tn), lambda i, j: (i, j)),
            pl.BlockSpec((1, tn), lambda i, j: (0, j)),
        ],
        out_specs=pl.BlockSpec((tm, tn), lambda i, j: (i, j)),
        out_shape=jax.ShapeDtypeStruct((t, n), F32),
        compiler_params=_cparams(("arbitrary", "arbitrary")),
        name="proj_res",
    )(a, w, x, gate)


DSA_TN = 512
DSA_HEADS_PER_TILE = DSA_TN // B_HEAD_DIM
DSA_Q_TILES = B_Q_W // DSA_TN
DSA_KV_TILE = DSA_Q_TILES
DSA_QI_TILE0 = DSA_KV_TILE + B_KV_RANK // DSA_TN
DSA_QI_TILES = B_QI_W // DSA_TN
DSA_KI_TILE = DSA_QI_TILE0 + DSA_QI_TILES


def _dsa_front_kernel(x_ref, sh_ref, sc_ref, w_ref, kvn_ref, kin_ref,
                      q_ref, qi_ref, ckv_ref, ckvt_ref, kidx_ref, wrow_ref, h_scr):
    j = pl.program_id(1)
    nbk = x_ref.shape[0] // Q_BLOCK

    @pl.when(j == 0)
    def _():
        h_scr[...] = _modulate(x_ref[...], sh_ref[...], sc_ref[...]).astype(BF16)

    y = _dot(h_scr[...], w_ref[...])

    def put_heads(ref):
        for b in range(nbk):
            for h in range(DSA_HEADS_PER_TILE):
                ref[b, h] = y[b * Q_BLOCK:(b + 1) * Q_BLOCK,
                              h * B_HEAD_DIM:(h + 1) * B_HEAD_DIM].astype(BF16)

    @pl.when(j < DSA_Q_TILES)
    def _():
        put_heads(q_ref)

    @pl.when(j == DSA_KV_TILE)
    def _():
        ckv = y * lax.rsqrt(jnp.mean(y * y, axis=-1, keepdims=True) + RMS_EPS) * kvn_ref[...]
        ckv_ref[...] = ckv.astype(BF16)
        ckvt_ref[...] = ckv.T.astype(BF16)

    @pl.when((j >= DSA_QI_TILE0) & (j < DSA_KI_TILE))
    def _():
        put_heads(qi_ref)

    @pl.when(j == DSA_KI_TILE)
    def _():
        kidx = y[:, :B_IDX_DIM]
        kidx = kidx * lax.rsqrt(jnp.mean(kidx * kidx, axis=-1, keepdims=True) + RMS_EPS) * kin_ref[...]
        kidx_ref[...] = kidx.astype(BF16)
        wi = y[:, B_IDX_DIM:B_IDX_DIM + LANES] * B_IDX_W_SCALE
        for b in range(nbk):
            wrow_ref[b] = wi[b * Q_BLOCK:(b + 1) * Q_BLOCK, :].T[:B_IDX_HEADS, :]


def _dsa_front(x, shift, scale, w_in, kv_norm, kidx_norm):
    t, d = x.shape
    assert B_HEAD_DIM == B_IDX_DIM and B_KV_RANK == DSA_TN and w_in.shape[1] <= (DSA_KI_TILE + 1) * DSA_TN
    n_tiles = DSA_KI_TILE + 1
    w = jnp.pad(w_in, ((0, 0), (0, n_tiles * DSA_TN - w_in.shape[1]))).astype(BF16)
    tm = 512
    nbk = tm // Q_BLOCK
    nb = t // Q_BLOCK
    row = lambda i, j: (0, 0)
    hpt = DSA_HEADS_PER_TILE
    return pl.pallas_call(
        _dsa_front_kernel,
        grid=(t // tm, n_tiles),
        in_specs=[
            pl.BlockSpec((tm, d), lambda i, j: (i, 0)),
            pl.BlockSpec((1, d), row),
            pl.BlockSpec((1, d), row),
            pl.BlockSpec((d, DSA_TN), lambda i, j: (0, j)),
            pl.BlockSpec((1, B_KV_RANK), row),
            pl.BlockSpec((1, B_IDX_DIM), row),
        ],
        out_specs=[
            pl.BlockSpec((nbk, hpt, Q_BLOCK, B_HEAD_DIM),
                         lambda i, j: (i, jnp.minimum(j, DSA_Q_TILES - 1), 0, 0)),
            pl.BlockSpec((nbk, hpt, Q_BLOCK, B_IDX_DIM),
                         lambda i, j: (i, jnp.clip(j - DSA_QI_TILE0, 0, DSA_QI_TILES - 1), 0, 0)),
            pl.BlockSpec((tm, B_KV_RANK), lambda i, j: (i, 0)),
            pl.BlockSpec((B_KV_RANK, tm), lambda i, j: (0, i)),
            pl.BlockSpec((tm, B_IDX_DIM), lambda i, j: (i, 0)),
            pl.BlockSpec((nbk, B_IDX_HEADS, Q_BLOCK), lambda i, j: (i, 0, 0)),
        ],
        out_shape=[
            jax.ShapeDtypeStruct((nb, B_HEADS, Q_BLOCK, B_HEAD_DIM), BF16),
            jax.ShapeDtypeStruct((nb, B_IDX_HEADS, Q_BLOCK, B_IDX_DIM), BF16),
            jax.ShapeDtypeStruct((t, B_KV_RANK), BF16),
            jax.ShapeDtypeStruct((B_KV_RANK, t), BF16),
            jax.ShapeDtypeStruct((t, B_IDX_DIM), BF16),
            jax.ShapeDtypeStruct((nb, B_IDX_HEADS, Q_BLOCK), F32),
        ],
        scratch_shapes=[pltpu.VMEM((tm, d), BF16)],
        compiler_params=_cparams(("arbitrary", "arbitrary")),
        name="dsa_front",
    )(x, shift, scale, w, kv_norm, kidx_norm)


def _rwkv_front_kernel(has_vres, x_ref, sh_ref, sc_ref, mu_ref, wrkv_ref, w1_ref, a1_ref, g1_ref,
                       *rest):
    if has_vres:
        v1_ref, rkv_ref, zw_ref, za_ref, zg_ref, zv_ref, carry = rest
    else:
        rkv_ref, zw_ref, za_ref, zg_ref, carry = rest
    i = pl.program_id(0)
    tm = x_ref.shape[0]

    @pl.when(i == 0)
    def _():
        carry[...] = jnp.zeros_like(carry)

    h = _modulate(x_ref[...], sh_ref[...], sc_ref[...])
    prev_last = carry[7:8, :]
    rolled = pltpu.roll(h, 1, axis=0)
    rows = lax.broadcasted_iota(I32, h.shape, 0)
    h_prev = jnp.where(rows == 0, prev_last, rolled)
    carry[...] = h[tm - 8:tm, :]
    dx = h_prev - h

    def mix(j):
        return (h + dx * mu_ref[j:j + 1, :]).astype(BF16)

    xv = mix(3)
    rkv_ref[0] = _dot(mix(0), wrkv_ref[0])
    rkv_ref[1] = _dot(mix(2), wrkv_ref[1])
    rkv_ref[2] = _dot(xv, wrkv_ref[2])
    zw_ref[...] = jnp.tanh(_dot(mix(1), w1_ref[...])).astype(BF16)
    za_ref[...] = _dot(mix(4), a1_ref[...]).astype(BF16)
    zg_ref[...] = _sigmoid(_dot(mix(5), g1_ref[...])).astype(BF16)
    if has_vres:
        zv_ref[...] = _dot(xv, v1_ref[...]).astype(BF16)


def _pad_cols(w):
    r = w.shape[1]
    return jnp.pad(w, ((0, 0), (0, -(-r // LANES) * LANES - r))).astype(BF16)


def _pad_rows(w):
    r = w.shape[0]
    return jnp.pad(w, ((0, -(-r // LANES) * LANES - r), (0, 0))).astype(BF16)


def _rwkv_front(x, shift, scale, mu, w_rkv, w1, a1, g1, v1):
    t, d = x.shape
    tm = 256
    has_vres = v1 is not None
    once = pl.Buffered(1)
    lora_ws = [w1, a1, g1] + ([v1] if has_vres else [])
    const2 = lambda i: (0, 0)
    in_specs = [
        pl.BlockSpec((tm, d), lambda i: (i, 0)),
        pl.BlockSpec((1, d), const2),
        pl.BlockSpec((1, d), const2),
        pl.BlockSpec((6, d), const2),
        pl.BlockSpec((3, d, d), lambda i: (0, 0, 0), pipeline_mode=once),
    ] + [pl.BlockSpec(w.shape, const2, pipeline_mode=once) for w in lora_ws]
    out_specs = [pl.BlockSpec((3, tm, d), lambda i: (0, i, 0))]
    out_specs += [pl.BlockSpec((tm, w.shape[1]), lambda i: (i, 0)) for w in lora_ws]
    out_shape = [jax.ShapeDtypeStruct((3, t, d), F32)]
    out_shape += [jax.ShapeDtypeStruct((t, w.shape[1]), BF16) for w in lora_ws]
    return pl.pallas_call(
        functools.partial(_rwkv_front_kernel, has_vres),
        grid=(t // tm,),
        in_specs=in_specs,
        out_specs=out_specs,
        out_shape=out_shape,
        scratch_shapes=[pltpu.VMEM((8, d), F32)],
        compiler_params=_cparams(("arbitrary",)),
        name="rwkv_front",
    )(x, shift, scale, mu, w_rkv, *lora_ws)


def _split2(x):
    hi = x.astype(BF16)
    lo = (x - hi.astype(F32)).astype(BF16)
    return hi, lo


def _bdot(a, b):
    return lax.dot_general(a, b, (((2,), (1,)), ((0,), (0,))), preferred_element_type=F32)


def _bdot_nt(a, b):
    return lax.dot_general(a, b, (((2,), (2,)), ((0,), (0,))), preferred_element_type=F32)


def _bdot_tn(a, b):
    return lax.dot_general(a, b, (((1,), (1,)), ((0,), (0,))), preferred_element_type=F32)


def _rwkv_chunk_kernel(has_vres, n_chunks, *refs):
    if has_vres:
        (r_ref, k_ref, v_ref, zw_ref, za_ref, zg_ref, zv_ref, vf_ref,
         w2_ref, a2_ref, g2_ref, v2_ref,
         w0_ref, a0_ref, kk_ref, ka_ref, rk_ref, lnw_ref, lnb_ref, v0_ref,
         o_ref, s_scr) = refs
    else:
        (r_ref, k_ref, v_ref, zw_ref, za_ref, zg_ref,
         w2_ref, a2_ref, g2_ref,
         w0_ref, a0_ref, kk_ref, ka_ref, rk_ref, lnw_ref, lnb_ref,
         o_ref, s_scr) = refs

    L = CHUNK
    W = o_ref.shape[1]
    NH = W // A_HEAD
    G = n_chunks
    R = G * L

    @pl.when(pl.program_id(1) == 0)
    def _():
        s_scr[...] = jnp.zeros_like(s_scr)

    lane = lax.broadcasted_iota(I32, (1, W), 1)
    head_masks = [((lane // A_HEAD) == e).astype(F32) for e in range(NH)]
    rr = lax.broadcasted_iota(I32, (L, L), 0)
    cc = lax.broadcasted_iota(I32, (L, L), 1)
    rr2 = lax.broadcasted_iota(I32, (L, 2 * L), 0)
    cc2 = lax.broadcasted_iota(I32, (L, 2 * L), 1) % L
    low_strict2 = (rr2 > cc2)[None]
    low_incl2 = (rr2 >= cc2)[None]
    eye_l = (rr == cc).astype(F32)[None]
    diag_blk = ((rr >> 3) == (cc >> 3))[None]
    merge_blks = [(((rr >> (s + 1)) == (cc >> (s + 1))) & ((rr >> s) > (cc >> s)))[None]
                  for s in (3, 4, 5)]
    r2 = lax.broadcasted_iota(I32, (W, W), 0)
    c2 = lax.broadcasted_iota(I32, (W, W), 1)
    same_head = (r2 // A_HEAD) == (c2 // A_HEAD)
    seg_ones = same_head.astype(BF16)
    r3 = lax.broadcasted_iota(I32, (R, R), 0)
    c3 = lax.broadcasted_iota(I32, (R, R), 1)
    tri_ones = (((r3 // L) == (c3 // L)) & (r3 >= c3)).astype(BF16)

    def segsum(x):
        hi, lo = _split2(x)
        return _dot(hi, seg_ones) + _dot(lo, seg_ones)

    r = r_ref[0]
    k = k_ref[0]
    v = v_ref[0]
    z = -(w0_ref[...] + _dot(zw_ref[...], w2_ref[...]))
    softplus = jnp.maximum(z, 0.0) + jnp.log(1.0 + jnp.exp(-jnp.abs(z)))
    logw = -jnp.exp(-softplus - 0.5)
    a = _sigmoid(a0_ref[...] + _dot(za_ref[...], a2_ref[...]))
    if has_vres:
        v = v + (vf_ref[0] - v) * _sigmoid(v0_ref[...] + _dot(zv_ref[...], v2_ref[...]))
    kk = k * kk_ref[...]
    kk = kk / jnp.maximum(jnp.sqrt(segsum(kk * kk)), 1e-12)
    k = k * (1.0 + (a - 1.0) * ka_ref[...])

    h1, h2 = _split2(logw)
    cum = _dot(tri_ones, h1) + _dot(tri_ones, h2)
    w_in = jnp.exp(cum)
    w_inv = jnp.exp(-cum)
    a_t = -kk * jnp.exp(cum - logw)
    b_t = kk * a * w_inv
    k_t = k * w_inv
    r_t = r * w_in

    def c3d(x):
        return x.reshape(G, L, W)

    def per_head(x):
        x3 = c3d(x)
        return jnp.concatenate([x3 * hm for hm in head_masks], axis=0).astype(BF16)

    def head_sum(x):
        out = x[:G]
        for e in range(1, NH):
            out = out + x[e * G:(e + 1) * G]
        return out

    a_e = per_head(a_t)
    r_e = per_head(r_t)
    v_e = per_head(v)
    bk_3 = jnp.concatenate([c3d(b_t), c3d(k_t)], axis=1).astype(BF16)
    bk_e = jnp.concatenate([bk_3] * NH, axis=0)
    zeros_e = jnp.zeros_like(v_e)

    x_a = jnp.where(low_strict2, _bdot_nt(a_e, bk_e), 0.0)
    x_r = jnp.where(low_incl2, _bdot_nt(r_e, bk_e), 0.0).astype(BF16)
    x_ab = x_a[:, :, :L]

    xp = jnp.where(diag_blk, x_ab, 0.0)
    tm = eye_l + xp
    for _ in range(2):
        xb = xp.astype(BF16)
        xp = _bdot(xb, xb)
        tm = tm + _bdot(tm.astype(BF16), xp.astype(BF16))
    for off_blk in merge_blks:
        tb = tm.astype(BF16)
        x_off = jnp.where(off_blk, x_ab, 0.0).astype(BF16)
        tm = tm + _bdot(_bdot(tb, x_off).astype(BF16), tb)
    tb = tm.astype(BF16)

    p_e = _bdot(tb, a_e)
    tx = _bdot(tb, x_a.astype(BF16)).astype(BF16)
    q_e = _bdot(tx, jnp.concatenate([zeros_e, v_e], axis=1))
    p_eb = p_e.astype(BF16)
    q_eb = q_e.astype(BF16)
    rp_e = _bdot(x_r, jnp.concatenate([p_eb, zeros_e], axis=1))
    y0_e = _bdot(x_r, jnp.concatenate([q_eb, v_e], axis=1))
    p_sum = head_sum(p_e)
    q_sum = head_sum(q_e)
    rp = (c3d(r_t) + head_sum(rp_e)).astype(BF16)
    y0 = head_sum(y0_e)

    b_3 = bk_3[:, :L, :]
    w_last = c3d(w_in)[:, L - 1:L, :]
    m_lr = (jnp.where(same_head[None], _bdot_tn(p_sum.astype(BF16), b_3), 0.0) * w_last).astype(BF16)
    qv = jnp.concatenate([q_sum, c3d(v)], axis=1).astype(BF16)
    c_bd = jnp.where(same_head[None], _bdot_tn(qv, bk_3), 0.0) * w_last

    s = s_scr[...]
    ys = []
    for c in range(G):
        s_b = s.astype(BF16)
        ys.append(_dot_nt(rp[c], s_b) + y0[c])
        s = s * w_last[c] + _dot(s_b, m_lr[c]) + c_bd[c]
    s_scr[...] = s
    y = jnp.concatenate(ys, axis=0)

    mean = segsum(y) * (1.0 / A_HEAD)
    dlt = y - mean
    var = segsum(dlt * dlt) * (1.0 / A_HEAD)
    yn = dlt * lax.rsqrt(var + A_GN_EPS) * lnw_ref[...] + lnb_ref[...]
    bonus = segsum(r * k * rk_ref[...]) * v
    o_ref[...] = ((yn + bonus) * _dot(zg_ref[...], g2_ref[...])).astype(BF16)


def _rwkv_chunks(rkv, zs, ups, rkv_first, w0, a0, k_k, k_a, r_k, ln_w, ln_b, v0):
    _, t, d = rkv.shape
    n_chunks = 8
    tm = n_chunks * CHUNK
    has_vres = len(zs) == 4
    blk = lambda p, c: (c, p)
    prm = lambda p, c: (0, p)
    w = RWKV_LANES
    slab = lambda j: pl.BlockSpec((1, tm, w), lambda p, c: (j, c, p))
    in_specs = [slab(0), slab(1), slab(2)]
    in_specs += [pl.BlockSpec((tm, z.shape[1]), lambda p, c: (c, 0)) for z in zs]
    if has_vres:
        in_specs += [slab(2)]
    in_specs += [pl.BlockSpec((u.shape[0], w), prm) for u in ups]
    in_specs += [pl.BlockSpec((1, w), prm)] * (8 if has_vres else 7)
    args = [rkv, rkv, rkv] + list(zs)
    if has_vres:
        args += [rkv_first]
    args += list(ups)
    args += [w0, a0, k_k, k_a, r_k, ln_w, ln_b]
    if has_vres:
        args += [v0]
    return pl.pallas_call(
        functools.partial(_rwkv_chunk_kernel, has_vres, n_chunks),
        grid=(d // w, t // tm),
        in_specs=in_specs,
        out_specs=pl.BlockSpec((tm, w), blk),
        out_shape=jax.ShapeDtypeStruct((t, d), BF16),
        scratch_shapes=[pltpu.VMEM((w, w), F32)],
        compiler_params=_cparams(("arbitrary", "arbitrary")),
        name="rwkv_chunks",
    )(*args)


def _dsa_attn_kernel(topk, tk, idx_bits,
                     q_ref, qi_ref, wrow_ref, kidx_ref, ckv_ref, ckvt_ref, wuk_ref, wuv_ref, o_ref,
                     key_scr, khi_scr, klo_scr, qlat_scr, acc_scr, m_scr, l_scr, j0_scr):
    i = pl.program_id(0)
    QB = Q_BLOCK
    cw = ATT_HEADS_PER_STEP * QB
    n_kt = (i * QB + QB + tk - 1) // tk
    qpos = i * QB + lax.broadcasted_iota(I32, (1, QB), 1)
    limit = (qpos // CHUNK + 1) * CHUNK

    def key_pos(off):
        return off + lax.broadcasted_iota(I32, (tk, QB), 0)

    for h in range(B_HEADS):
        ql = _dot(q_ref[0, h], wuk_ref[h]) * (B_SCALE * LOG2_E)
        qlat_scr[h * QB:(h + 1) * QB, :] = ql.astype(BF16)

    qi = qi_ref[0].reshape(B_IDX_HEADS * QB, B_IDX_DIM)
    wrow = wrow_ref[0]

    def score_body(kt, carry):
        off = pl.multiple_of(kt * tk, tk)
        rel = jnp.maximum(_dot_nt(kidx_ref[pl.ds(off, tk), :], qi), 0.0)
        sc = rel[:, 0:QB] * wrow[0:1, :]
        for h in range(1, B_IDX_HEADS):
            sc = sc + rel[:, h * QB:(h + 1) * QB] * wrow[h:h + 1, :]
        sc = jnp.where(sc == 0.0, 0.0, sc)
        bits = pltpu.bitcast(sc, I32)
        key = bits ^ ((bits >> 31) & INT_MAX)
        key = jnp.where(key_pos(off) < limit, key, INT_MIN)
        key_scr[pl.ds(off, tk), :] = key
        khi_scr[pl.ds(off, tk), :] = (key >> 16).astype(I16)
        return carry

    lax.fori_loop(0, n_kt, score_body, 0)

    def count_keys(pred):
        def body(kt, acc):
            off = pl.multiple_of(kt * tk, tk)
            hit = jnp.where(pred(key_scr[pl.ds(off, tk), :], off), 1.0, 0.0)
            return acc + jnp.sum(hit.reshape(tk // (8 * COUNT_WAYS), COUNT_WAYS, 8, QB), axis=0)
        acc = lax.fori_loop(0, n_kt, body, jnp.zeros((COUNT_WAYS, 8, QB), F32))
        return jnp.sum(jnp.sum(acc, axis=0), axis=0, keepdims=True)

    def count16(ref, cand, strict):
        cand16 = cand.astype(I16)
        one = jnp.ones((tk, QB), I16)
        zero = jnp.zeros((tk, QB), I16)

        def body(kt, acc):
            off = pl.multiple_of(kt * tk, tk)
            half = ref[pl.ds(off, tk), :]
            hit = jnp.where(half > cand16 if strict else half >= cand16, one, zero)
            parts = hit.reshape(tk // (16 * COUNT_WAYS), COUNT_WAYS, 16, QB)
            for i in range(parts.shape[0]):
                acc = acc + parts[i]
            return acc
        acc = lax.fori_loop(0, n_kt, body, jnp.zeros((COUNT_WAYS, 16, QB), I16))
        return jnp.sum(jnp.sum(acc.astype(F32), axis=0), axis=0, keepdims=True)

    HALF = 1 << 15

    def hi_body(b, st):
        thr_h, cnt_ge = st
        cand = thr_h | jnp.left_shift(jnp.int32(1), 15 - b)
        cnt = count16(khi_scr, cand - HALF, False)
        take = cnt >= topk
        return jnp.where(take, cand, thr_h), jnp.where(take, cnt, cnt_ge)

    n_scanned = jnp.zeros((1, QB), F32) + (n_kt * tk).astype(F32)
    thr_h, cnt_ge = lax.fori_loop(0, 16, hi_body, (jnp.zeros((1, QB), I32), n_scanned))
    thr_hs = thr_h - HALF
    cnt_above = count16(khi_scr, thr_hs, True)

    def lo_fill(kt, carry):
        off = pl.multiple_of(kt * tk, tk)
        key = key_scr[pl.ds(off, tk), :]
        low = jnp.where((key >> 16) == thr_hs, (key & 0xFFFF) - HALF, -HALF)
        klo_scr[pl.ds(off, tk), :] = low.astype(I16)
        return carry

    lax.fori_loop(0, n_kt, lo_fill, 0)

    def lo_body(b, st):
        thr_l, cnt_ge = st
        cand = thr_l | jnp.left_shift(jnp.int32(1), 15 - b)
        cnt = cnt_above + count16(klo_scr, cand - HALF, False)
        take = cnt >= topk
        return jnp.where(take, cand, thr_l), jnp.where(take, cnt, cnt_ge)

    thr_l, cnt_ge = lax.fori_loop(0, 16, lo_body, (jnp.zeros((1, QB), I32), cnt_ge))
    thr = thr_hs * 65536 + thr_l

    j0_scr[...] = jnp.full((1, QB), INT_MAX, I32)
    excess = jnp.max(jnp.where(thr > INT_MIN, cnt_ge, 0.0)) > topk

    @pl.when(excess)
    def _():
        need = topk - count_keys(lambda key, off: key > thr)

        def tie_body(b, j0):
            cand = j0 | jnp.left_shift(jnp.int32(1), idx_bits - 1 - b)
            cnt = count_keys(lambda key, off: (key == thr) & (key_pos(off) < cand))
            return jnp.where(cnt < need, cand, j0)
        j0_scr[...] = lax.fori_loop(0, idx_bits, tie_body, jnp.zeros((1, QB), I32))

    j0 = j0_scr[...]

    m_scr[...] = jnp.full(m_scr.shape, NEG_BIG, F32)
    l_scr[...] = jnp.zeros_like(l_scr)
    acc_scr[...] = jnp.zeros_like(acc_scr)

    def att_tile(off, size):
        key = key_scr[pl.ds(off, size), :]
        kpos = off + lax.broadcasted_iota(I32, (size, QB), 0)
        sel = ((key > thr) | ((key == thr) & (kpos <= j0))) & (key > INT_MIN)
        bias = jnp.where(sel, 0.0, NEG_BIG)
        bias = jnp.concatenate([bias] * ATT_HEADS_PER_STEP, axis=1)
        ckv_t = ckv_ref[pl.ds(off, size), :]
        ckvt_t = ckvt_ref[:, pl.ds(off, size)]
        n_groups = B_HEADS // ATT_HEADS_PER_STEP

        def logits(c):
            return _dot_nt(ckv_t, qlat_scr[c * cw:(c + 1) * cw, :]) + bias

        s_next = logits(0)
        for c in range(n_groups):
            cols = slice(c * cw, (c + 1) * cw)
            s = s_next
            if c + 1 < n_groups:
                s_next = logits(c + 1)
            m_old = m_scr[:, cols]
            m_new = jnp.maximum(m_old, jnp.max(s, axis=0, keepdims=True))
            alpha = jnp.exp2(m_old - m_new)
            p = jnp.exp2(s - m_new)
            l_scr[:, cols] = alpha * l_scr[:, cols] + jnp.sum(p, axis=0, keepdims=True)
            acc_scr[:, cols] = alpha * acc_scr[:, cols] + _dot(ckvt_t, p.astype(BF16))
            m_scr[:, cols] = m_new

    def att_pair(j, carry):
        att_tile(pl.multiple_of(j * (2 * tk), 2 * tk), 2 * tk)
        return carry

    lax.fori_loop(0, n_kt // 2, att_pair, 0)

    @pl.when(n_kt % 2 == 1)
    def _():
        att_tile(pl.multiple_of((n_kt - 1) * tk, tk), tk)

    o_lat_t = (acc_scr[...] / l_scr[...]).astype(BF16)
    for h in range(B_HEADS):
        oh = _dot_tn(o_lat_t[:, h * QB:(h + 1) * QB], wuv_ref[h])
        o_ref[:, h * B_HEAD_DIM:(h + 1) * B_HEAD_DIM] = oh.astype(BF16)


def _dsa_attn(q, qi, wrow, kidx, ckv, ckvt, w_uk, w_uv):
    nb = q.shape[0]
    t = kidx.shape[0]
    tk = 512
    topk = min(TOPK_MAX, t // 4)
    idx_bits = max(1, (t - 1).bit_length())
    const2 = lambda i: (0, 0)
    const3 = lambda i: (0, 0, 0)
    hq = B_HEADS * Q_BLOCK
    once = pl.Buffered(1)
    return pl.pallas_call(
        functools.partial(_dsa_attn_kernel, topk, tk, idx_bits),
        grid=(nb,),
        in_specs=[
            pl.BlockSpec((1, B_HEADS, Q_BLOCK, B_HEAD_DIM), lambda i: (i, 0, 0, 0)),
            pl.BlockSpec((1, B_IDX_HEADS, Q_BLOCK, B_IDX_DIM), lambda i: (i, 0, 0, 0)),
            pl.BlockSpec((1, B_IDX_HEADS, Q_BLOCK), lambda i: (i, 0, 0)),
            pl.BlockSpec((t, B_IDX_DIM), const2, pipeline_mode=once),
            pl.BlockSpec((t, B_KV_RANK), const2, pipeline_mode=once),
            pl.BlockSpec((B_KV_RANK, t), const2, pipeline_mode=once),
            pl.BlockSpec((B_HEADS, B_HEAD_DIM, B_KV_RANK), const3, pipeline_mode=once),
            pl.BlockSpec((B_HEADS, B_KV_RANK, B_HEAD_DIM), const3, pipeline_mode=once),
        ],
        out_specs=pl.BlockSpec((Q_BLOCK, B_Q_W), lambda i: (i, 0)),
        out_shape=jax.ShapeDtypeStruct((t, B_Q_W), BF16),
        scratch_shapes=[
            pltpu.VMEM((t, Q_BLOCK), I32),
            pltpu.VMEM((t, Q_BLOCK), I16),
            pltpu.VMEM((t, Q_BLOCK), I16),
            pltpu.VMEM((hq, B_KV_RANK), BF16),
            pltpu.VMEM((B_KV_RANK, hq), F32),
            pltpu.VMEM((1, hq), F32),
            pltpu.VMEM((1, hq), F32),
            pltpu.VMEM((1, Q_BLOCK), I32),
        ],
        compiler_params=_cparams(("arbitrary",)),
        name="dsa_attn",
    )(q, qi, wrow, kidx, ckv, ckvt, w_uk, w_uv)


def _final_norm_kernel(x_ref, w_ref, o_ref):
    x = x_ref[...]
    o_ref[...] = x * lax.rsqrt(jnp.mean(x * x, axis=-1, keepdims=True) + RMS_EPS) * w_ref[...]


def _final_norm(x, w):
    t, d = x.shape
    tm = 512
    return pl.pallas_call(
        _final_norm_kernel,
        grid=(t // tm,),
        in_specs=[pl.BlockSpec((tm, d), lambda i: (i, 0)), pl.BlockSpec((1, d), lambda i: (0, 0))],
        out_specs=pl.BlockSpec((tm, d), lambda i: (i, 0)),
        out_shape=jax.ShapeDtypeStruct((t, d), F32),
        compiler_params=_cparams(("arbitrary",)),
        name="final_norm",
    )(x, w)


def _row(v):
    return v.reshape(1, -1)


def _rwkv_layer(x, shift, scale, gate, rkv_first, mu, w_rkv, w0, w1, w2, a0, a1, a2,
                g1, g2, k_k, k_a, r_k, ln_w, ln_b, w_o, vres):
    ups = [_pad_rows(w2), _pad_rows(a2), _pad_rows(g2)]
    if vres is None:
        v0, v1 = None, None
    else:
        v0, v1, v2 = vres
        v0, v1 = _row(v0), _pad_cols(v1)
        ups.append(_pad_rows(v2))
    outs = _rwkv_front(x, shift, scale, mu, w_rkv.astype(BF16), _pad_cols(w1), _pad_cols(a1),
                       _pad_cols(g1), v1)
    rkv, zs = outs[0], outs[1:]
    if vres is None:
        rkv_first = rkv
    y = _rwkv_chunks(rkv, zs, ups, rkv_first, _row(w0), _row(a0), _row(k_k), _row(k_a),
                     _row(r_k), _row(ln_w), _row(ln_b), v0)
    return _proj_res(y, w_o.astype(BF16), x, gate), rkv_first


def _dsa_layer(x, shift, scale, gate, w_in, kv_norm, kidx_norm, w_uk, w_uv, w_o):
    q, qi, ckv, ckvt, kidx, wrow = _dsa_front(x, shift, scale, w_in, _row(kv_norm), _row(kidx_norm))
    o = _dsa_attn(q, qi, wrow, kidx, ckv, ckvt, w_uk.astype(BF16), w_uv.astype(BF16))
    return _proj_res(o, w_o.astype(BF16), x, gate)


def kernel(x, c, ada_w, ada_b, a_mu, a_w_rkv, a_w0, a_w1, a_w2, a_a0, a_a1, a_a2, a_v0, a_v1, a_v2, a_g1, a_g2, a_k_k, a_k_a, a_r_k, a_ln_w, a_ln_b, a_w_o, b_w_in, b_kv_norm, b_kidx_norm, b_w_uk, b_w_uv, b_w_o, f_w1, f_w3, f_w2, final_norm):
    b, t, d = x.shape
    assert b == 1 and d == D_MODEL
    mod_all = _ada(c, ada_w, ada_b)
    xs = x.reshape(t, d)
    v_first = None
    for i in range(DEPTH):
        shift1, scale1, gate1, shift2, scale2, gate2 = (
            mod_all[i, :, s * d:(s + 1) * d] for s in range(6))
        j = i // 2
        if i % 2 == 0:
            vres = None if j == 0 else (a_v0[j - 1], a_v1[j - 1], a_v2[j - 1])
            xs, v_first = _rwkv_layer(
                xs, shift1, scale1, gate1, v_first, a_mu[j], a_w_rkv[j], a_w0[j], a_w1[j], a_w2[j],
                a_a0[j], a_a1[j], a_a2[j], a_g1[j], a_g2[j], a_k_k[j], a_k_a[j], a_r_k[j],
                a_ln_w[j], a_ln_b[j], a_w_o[j], vres)
        else:
            xs = _dsa_layer(xs, shift1, scale1, gate1, b_w_in[j], b_kv_norm[j], b_kidx_norm[j],
                            b_w_uk[j], b_w_uv[j], b_w_o[j])
        xs = _ffn(xs, shift2, scale2, gate2, f_w1[i].astype(BF16), f_w3[i].astype(BF16),
                  f_w2[i].astype(BF16))
    return _final_norm(xs, _row(final_norm)).reshape(b, t, d)
```

```python
import functools

import jax
import jax.numpy as jnp
from jax import lax
from jax.experimental import pallas as pl
from jax.experimental.pallas import tpu as pltpu

F32 = jnp.float32
BF16 = jnp.bfloat16
I32 = jnp.int32

D_MODEL = 2048
DEPTH = 4
CHUNK = 64
RMS_EPS = 1e-6
A_HEAD = 64
A_HEADS = D_MODEL // A_HEAD
A_GN_EPS = A_HEAD * 1e-5
B_HEADS = 16
B_HEAD_DIM = 128
B_KV_RANK = 512
B_IDX_HEADS = 8
B_IDX_DIM = 128
TOPK_MAX = 256
Q_BLOCK = 128
B_Q_W = B_HEADS * B_HEAD_DIM
B_QI_W = B_IDX_HEADS * B_IDX_DIM
B_SCALE = B_HEAD_DIM ** -0.5
B_IDX_W_SCALE = (B_IDX_HEADS ** -0.5) * (B_IDX_DIM ** -0.5)
D_FF = 5632

LANES = 128
VMEM_LIMIT = 56 * 1024 * 1024
INT_MIN = -(2 ** 31)
INT_MAX = 2 ** 31 - 1
NEG_BIG = -1e30
LOG2_E = 1.4426950408889634
COUNT_WAYS = 8
ATT_HEADS_PER_STEP = 4
ATT_TILE_MULT = 4
RWKV_LANES = 256

_NT = (((1,), (1,)), ((), ()))
_TN = (((0,), (0,)), ((), ()))


def _cparams(sem):
    return pltpu.CompilerParams(dimension_semantics=sem, vmem_limit_bytes=VMEM_LIMIT)


def _dot(a, b):
    return jnp.dot(a, b, preferred_element_type=F32)


def _dot_nt(a, b):
    return lax.dot_general(a, b, _NT, preferred_element_type=F32)


def _dot_tn(a, b):
    return lax.dot_general(a, b, _TN, preferred_element_type=F32)


def _modulate(x, shift, scale):
    ms = jnp.mean(x * x, axis=-1, keepdims=True)
    return x * lax.rsqrt(ms + RMS_EPS) * (1.0 + scale) + shift


def _sigmoid(x):
    return 1.0 / (1.0 + jnp.exp(-x))


def _ada_kernel(c_ref, w_ref, b_ref, o_ref):
    c = c_ref[...]
    s = c * _sigmoid(c)
    o_ref[0] = jnp.sum(s * w_ref[0], axis=0, keepdims=True) + b_ref[0]


def _ada(c, ada_w, ada_b):
    depth, d, n = ada_w.shape
    tn = 1024
    return pl.pallas_call(
        _ada_kernel,
        grid=(depth, n // tn),
        in_specs=[
            pl.BlockSpec((d, 1), lambda l, j: (0, 0)),
            pl.BlockSpec((1, d, tn), lambda l, j: (l, 0, j)),
            pl.BlockSpec((1, 1, tn), lambda l, j: (l, 0, j)),
        ],
        out_specs=pl.BlockSpec((1, 1, tn), lambda l, j: (l, 0, j)),
        out_shape=jax.ShapeDtypeStruct((depth, 1, n), F32),
        compiler_params=_cparams(("arbitrary", "arbitrary")),
        name="ada_mod",
    )(c.reshape(d, 1), ada_w, ada_b.reshape(depth, 1, n))


def _ffn_kernel(x_ref, sh_ref, sc_ref, g_ref, w1_ref, w3_ref, w2_ref, o_ref, h_scr, acc_scr):
    f = pl.program_id(1)

    @pl.when(f == 0)
    def _():
        h_scr[...] = _modulate(x_ref[...], sh_ref[...], sc_ref[...]).astype(BF16)
        acc_scr[...] = jnp.zeros_like(acc_scr)

    h = h_scr[...]
    a = _dot(h, w1_ref[...])
    b = _dot(h, w3_ref[...])
    u = (a * _sigmoid(a) * b).astype(BF16)
    acc_scr[...] += _dot(u, w2_ref[...])

    @pl.when(f == pl.num_programs(1) - 1)
    def _():
        o_ref[...] = x_ref[...] + g_ref[...] * acc_scr[...]


def _ffn(x, shift, scale, gate, w1, w3, w2):
    t, d = x.shape
    dff = w1.shape[1]
    tm, tf = 512, 512
    row = lambda i, f: (0, 0)
    return pl.pallas_call(
        _ffn_kernel,
        grid=(t // tm, dff // tf),
        in_specs=[
            pl.BlockSpec((tm, d), lambda i, f: (i, 0)),
            pl.BlockSpec((1, d), row),
            pl.BlockSpec((1, d), row),
            pl.BlockSpec((1, d), row),
            pl.BlockSpec((d, tf), lambda i, f: (0, f)),
            pl.BlockSpec((d, tf), lambda i, f: (0, f)),
            pl.BlockSpec((tf, d), lambda i, f: (f, 0)),
        ],
        out_specs=pl.BlockSpec((tm, d), lambda i, f: (i, 0)),
        out_shape=jax.ShapeDtypeStruct((t, d), F32),
        scratch_shapes=[pltpu.VMEM((tm, d), BF16), pltpu.VMEM((tm, d), F32)],
        compiler_params=_cparams(("arbitrary", "arbitrary")),
        name="ffn",
    )(x, shift, scale, gate, w1, w3, w2)


def _proj_res_kernel(a_ref, w_ref, x_ref, g_ref, o_ref):
    o_ref[...] = x_ref[...] + g_ref[...] * _dot(a_ref[...], w_ref[...])


def _proj_res(a, w, x, gate):
    t, k = a.shape
    n = w.shape[1]
    tm, tn = 512, n
    return pl.pallas_call(
        _proj_res_kernel,
        grid=(t // tm, n // tn),
        in_specs=[
            pl.BlockSpec((tm, k), lambda i, j: (i, 0)),
            pl.BlockSpec((k, tn), lambda i, j: (0, j), pipeline_mode=pl.Buffered(1)),
            pl.BlockSpec((tm, tn), lambda i, j: (i, j)),
            pl.BlockSpec((1, tn), lambda i, j: (0, j)),
        ],
        out_specs=pl.BlockSpec((tm, tn), lambda i, j: (i, j)),
        out_shape=jax.ShapeDtypeStruct((t, n), F32),
        compiler_params=_cparams(("arbitrary", "arbitrary")),
        name="proj_res",
    )(a, w, x, gate)


DSA_TN = 512
DSA_HEADS_PER_TILE = DSA_TN // B_HEAD_DIM
DSA_Q_TILES = B_Q_W // DSA_TN
DSA_KV_TILE = DSA_Q_TILES
DSA_QI_TILE0 = DSA_KV_TILE + B_KV_RANK // DSA_TN
DSA_QI_TILES = B_QI_W // DSA_TN
DSA_KI_TILE = DSA_QI_TILE0 + DSA_QI_TILES


def _dsa_front_kernel(x_ref, sh_ref, sc_ref, w_ref, kvn_ref, kin_ref,
                      q_ref, qi_ref, ckv_ref, ckvt_ref, kidx_ref, wrow_ref, h_scr):
    j = pl.program_id(1)
    nbk = x_ref.shape[0] // Q_BLOCK

    @pl.when(j == 0)
    def _():
        h_scr[...] = _modulate(x_ref[...], sh_ref[...], sc_ref[...]).astype(BF16)

    y = _dot(h_scr[...], w_ref[...])

    def put_heads(ref):
        for b in range(nbk):
            for h in range(DSA_HEADS_PER_TILE):
                ref[b, h] = y[b * Q_BLOCK:(b + 1) * Q_BLOCK,
                              h * B_HEAD_DIM:(h + 1) * B_HEAD_DIM].astype(BF16)

    @pl.when(j < DSA_Q_TILES)
    def _():
        put_heads(q_ref)

    @pl.when(j == DSA_KV_TILE)
    def _():
        ckv = y * lax.rsqrt(jnp.mean(y * y, axis=-1, keepdims=True) + RMS_EPS) * kvn_ref[...]
        ckv_ref[...] = ckv.astype(BF16)
        ckvt_ref[...] = ckv.T.astype(BF16)

    @pl.when((j >= DSA_QI_TILE0) & (j < DSA_KI_TILE))
    def _():
        put_heads(qi_ref)

    @pl.when(j == DSA_KI_TILE)
    def _():
        kidx = y[:, :B_IDX_DIM]
        kidx = kidx * lax.rsqrt(jnp.mean(kidx * kidx, axis=-1, keepdims=True) + RMS_EPS) * kin_ref[...]
        kidx_ref[...] = kidx.astype(BF16)
        wi = y[:, B_IDX_DIM:B_IDX_DIM + LANES] * B_IDX_W_SCALE
        for b in range(nbk):
            wrow_ref[b] = wi[b * Q_BLOCK:(b + 1) * Q_BLOCK, :].T[:B_IDX_HEADS, :]


def _dsa_front(x, shift, scale, w_in, kv_norm, kidx_norm):
    t, d = x.shape
    assert B_HEAD_DIM == B_IDX_DIM and B_KV_RANK == DSA_TN and w_in.shape[1] <= (DSA_KI_TILE + 1) * DSA_TN
    n_tiles = DSA_KI_TILE + 1
    w = jnp.pad(w_in, ((0, 0), (0, n_tiles * DSA_TN - w_in.shape[1]))).astype(BF16)
    tm = 512
    nbk = tm // Q_BLOCK
    nb = t // Q_BLOCK
    row = lambda i, j: (0, 0)
    hpt = DSA_HEADS_PER_TILE
    return pl.pallas_call(
        _dsa_front_kernel,
        grid=(t // tm, n_tiles),
        in_specs=[
            pl.BlockSpec((tm, d), lambda i, j: (i, 0)),
            pl.BlockSpec((1, d), row),
            pl.BlockSpec((1, d), row),
            pl.BlockSpec((d, DSA_TN), lambda i, j: (0, j)),
            pl.BlockSpec((1, B_KV_RANK), row),
            pl.BlockSpec((1, B_IDX_DIM), row),
        ],
        out_specs=[
            pl.BlockSpec((nbk, hpt, Q_BLOCK, B_HEAD_DIM),
                         lambda i, j: (i, jnp.minimum(j, DSA_Q_TILES - 1), 0, 0)),
            pl.BlockSpec((nbk, hpt, Q_BLOCK, B_IDX_DIM),
                         lambda i, j: (i, jnp.clip(j - DSA_QI_TILE0, 0, DSA_QI_TILES - 1), 0, 0)),
            pl.BlockSpec((tm, B_KV_RANK), lambda i, j: (i, 0)),
            pl.BlockSpec((B_KV_RANK, tm), lambda i, j: (0, i)),
            pl.BlockSpec((tm, B_IDX_DIM), lambda i, j: (i, 0)),
            pl.BlockSpec((nbk, B_IDX_HEADS, Q_BLOCK), lambda i, j: (i, 0, 0)),
        ],
        out_shape=[
            jax.ShapeDtypeStruct((nb, B_HEADS, Q_BLOCK, B_HEAD_DIM), BF16),
            jax.ShapeDtypeStruct((nb, B_IDX_HEADS, Q_BLOCK, B_IDX_DIM), BF16),
            jax.ShapeDtypeStruct((t, B_KV_RANK), BF16),
            jax.ShapeDtypeStruct((B_KV_RANK, t), BF16),
            jax.ShapeDtypeStruct((t, B_IDX_DIM), BF16),
            jax.ShapeDtypeStruct((nb, B_IDX_HEADS, Q_BLOCK), F32),
        ],
        scratch_shapes=[pltpu.VMEM((tm, d), BF16)],
        compiler_params=_cparams(("arbitrary", "arbitrary")),
        name="dsa_front",
    )(x, shift, scale, w, kv_norm, kidx_norm)


def _rwkv_front_kernel(has_vres, x_ref, sh_ref, sc_ref, mu_ref, wrkv_ref, w1_ref, a1_ref, g1_ref,
                       *rest):
    if has_vres:
        v1_ref, rkv_ref, zw_ref, za_ref, zg_ref, zv_ref, carry = rest
    else:
        rkv_ref, zw_ref, za_ref, zg_ref, carry = rest
    i = pl.program_id(0)
    tm = x_ref.shape[0]

    @pl.when(i == 0)
    def _():
        carry[...] = jnp.zeros_like(carry)

    h = _modulate(x_ref[...], sh_ref[...], sc_ref[...])
    prev_last = carry[7:8, :]
    rolled = pltpu.roll(h, 1, axis=0)
    rows = lax.broadcasted_iota(I32, h.shape, 0)
    h_prev = jnp.where(rows == 0, prev_last, rolled)
    carry[...] = h[tm - 8:tm, :]
    dx = h_prev - h

    def mix(j):
        return (h + dx * mu_ref[j:j + 1, :]).astype(BF16)

    xv = mix(3)
    rkv_ref[0] = _dot(mix(0), wrkv_ref[0])
    rkv_ref[1] = _dot(mix(2), wrkv_ref[1])
    rkv_ref[2] = _dot(xv, wrkv_ref[2])
    zw_ref[...] = jnp.tanh(_dot(mix(1), w1_ref[...])).astype(BF16)
    za_ref[...] = _dot(mix(4), a1_ref[...]).astype(BF16)
    zg_ref[...] = _sigmoid(_dot(mix(5), g1_ref[...])).astype(BF16)
    if has_vres:
        zv_ref[...] = _dot(xv, v1_ref[...]).astype(BF16)


def _pad_cols(w):
    r = w.shape[1]
    return jnp.pad(w, ((0, 0), (0, -(-r // LANES) * LANES - r))).astype(BF16)


def _pad_rows(w):
    r = w.shape[0]
    return jnp.pad(w, ((0, -(-r // LANES) * LANES - r), (0, 0))).astype(BF16)


def _rwkv_front(x, shift, scale, mu, w_rkv, w1, a1, g1, v1):
    t, d = x.shape
    tm = 256
    has_vres = v1 is not None
    once = pl.Buffered(1)
    lora_ws = [w1, a1, g1] + ([v1] if has_vres else [])
    const2 = lambda i: (0, 0)
    in_specs = [
        pl.BlockSpec((tm, d), lambda i: (i, 0)),
        pl.BlockSpec((1, d), const2),
        pl.BlockSpec((1, d), const2),
        pl.BlockSpec((6, d), const2),
        pl.BlockSpec((3, d, d), lambda i: (0, 0, 0), pipeline_mode=once),
    ] + [pl.BlockSpec(w.shape, const2, pipeline_mode=once) for w in lora_ws]
    out_specs = [pl.BlockSpec((3, tm, d), lambda i: (0, i, 0))]
    out_specs += [pl.BlockSpec((tm, w.shape[1]), lambda i: (i, 0)) for w in lora_ws]
    out_shape = [jax.ShapeDtypeStruct((3, t, d), F32)]
    out_shape += [jax.ShapeDtypeStruct((t, w.shape[1]), BF16) for w in lora_ws]
    return pl.pallas_call(
        functools.partial(_rwkv_front_kernel, has_vres),
        grid=(t // tm,),
        in_specs=in_specs,
        out_specs=out_specs,
        out_shape=out_shape,
        scratch_shapes=[pltpu.VMEM((8, d), F32)],
        compiler_params=_cparams(("arbitrary",)),
        name="rwkv_front",
    )(x, shift, scale, mu, w_rkv, *lora_ws)


def _split2(x):
    hi = x.astype(BF16)
    lo = (x - hi.astype(F32)).astype(BF16)
    return hi, lo


def _bdot(a, b):
    return lax.dot_general(a, b, (((2,), (1,)), ((0,), (0,))), preferred_element_type=F32)


def _bdot_nt(a, b):
    return lax.dot_general(a, b, (((2,), (2,)), ((0,), (0,))), preferred_element_type=F32)


def _bdot_tn(a, b):
    return lax.dot_general(a, b, (((1,), (1,)), ((0,), (0,))), preferred_element_type=F32)


def _rwkv_chunk_kernel(has_vres, n_chunks, *refs):
    if has_vres:
        (r_ref, k_ref, v_ref, zw_ref, za_ref, zg_ref, zv_ref, vf_ref,
         w2_ref, a2_ref, g2_ref, v2_ref,
         w0_ref, a0_ref, kk_ref, ka_ref, rk_ref, lnw_ref, lnb_ref, v0_ref,
         o_ref, s_scr) = refs
    else:
        (r_ref, k_ref, v_ref, zw_ref, za_ref, zg_ref,
         w2_ref, a2_ref, g2_ref,
         w0_ref, a0_ref, kk_ref, ka_ref, rk_ref, lnw_ref, lnb_ref,
         o_ref, s_scr) = refs

    L = CHUNK
    W = o_ref.shape[1]
    NH = W // A_HEAD
    G = n_chunks
    R = G * L

    @pl.when(pl.program_id(1) == 0)
    def _():
        s_scr[...] = jnp.zeros_like(s_scr)

    lane = lax.broadcasted_iota(I32, (1, W), 1)
    head_masks = [((lane // A_HEAD) == e).astype(F32) for e in range(NH)]
    rr = lax.broadcasted_iota(I32, (L, L), 0)
    cc = lax.broadcasted_iota(I32, (L, L), 1)
    rr2 = lax.broadcasted_iota(I32, (L, 2 * L), 0)
    cc2 = lax.broadcasted_iota(I32, (L, 2 * L), 1) % L
    low_strict2 = (rr2 > cc2)[None]
    low_incl2 = (rr2 >= cc2)[None]
    eye_l = (rr == cc).astype(F32)[None]
    diag_blk = ((rr >> 3) == (cc >> 3))[None]
    merge_blks = [(((rr >> (s + 1)) == (cc >> (s + 1))) & ((rr >> s) > (cc >> s)))[None]
                  for s in (3, 4, 5)]
    r2 = lax.broadcasted_iota(I32, (W, W), 0)
    c2 = lax.broadcasted_iota(I32, (W, W), 1)
    same_head = (r2 // A_HEAD) == (c2 // A_HEAD)
    seg_ones = same_head.astype(BF16)
    r3 = lax.broadcasted_iota(I32, (R, R), 0)
    c3 = lax.broadcasted_iota(I32, (R, R), 1)
    tri_ones = (((r3 // L) == (c3 // L)) & (r3 >= c3)).astype(BF16)

    def segsum(x):
        hi, lo = _split2(x)
        return _dot(hi, seg_ones) + _dot(lo, seg_ones)

    r = r_ref[0]
    k = k_ref[0]
    v = v_ref[0]
    z = -(w0_ref[...] + _dot(zw_ref[...], w2_ref[...]))
    softplus = jnp.maximum(z, 0.0) + jnp.log(1.0 + jnp.exp(-jnp.abs(z)))
    logw = -jnp.exp(-softplus - 0.5)
    a = _sigmoid(a0_ref[...] + _dot(za_ref[...], a2_ref[...]))
    if has_vres:
        v = v + (vf_ref[0] - v) * _sigmoid(v0_ref[...] + _dot(zv_ref[...], v2_ref[...]))
    kk = k * kk_ref[...]
    kk = kk / jnp.maximum(jnp.sqrt(segsum(kk * kk)), 1e-12)
    k = k * (1.0 + (a - 1.0) * ka_ref[...])

    h1, h2 = _split2(logw)
    cum = _dot(tri_ones, h1) + _dot(tri_ones, h2)
    w_in = jnp.exp(cum)
    w_inv = jnp.exp(-cum)
    a_t = -kk * jnp.exp(cum - logw)
    b_t = kk * a * w_inv
    k_t = k * w_inv
    r_t = r * w_in

    def c3d(x):
        return x.reshape(G, L, W)

    def per_head(x):
        x3 = c3d(x)
        return jnp.concatenate([x3 * hm for hm in head_masks], axis=0).astype(BF16)

    def head_sum(x):
        out = x[:G]
        for e in range(1, NH):
            out = out + x[e * G:(e + 1) * G]
        return out

    a_e = per_head(a_t)
    r_e = per_head(r_t)
    v_e = per_head(v)
    bk_3 = jnp.concatenate([c3d(b_t), c3d(k_t)], axis=1).astype(BF16)
    bk_e = jnp.concatenate([bk_3] * NH, axis=0)
    zeros_e = jnp.zeros_like(v_e)

    x_a = jnp.where(low_strict2, _bdot_nt(a_e, bk_e), 0.0)
    x_r = jnp.where(low_incl2, _bdot_nt(r_e, bk_e), 0.0).astype(BF16)
    x_ab = x_a[:, :, :L]

    xp = jnp.where(diag_blk, x_ab, 0.0)
    tm = eye_l + xp
    for _ in range(2):
        xb = xp.astype(BF16)
        xp = _bdot(xb, xb)
        tm = tm + _bdot(tm.astype(BF16), xp.astype(BF16))
    for off_blk in merge_blks:
        tb = tm.astype(BF16)
        x_off = jnp.where(off_blk, x_ab, 0.0).astype(BF16)
        tm = tm + _bdot(_bdot(tb, x_off).astype(BF16), tb)
    tb = tm.astype(BF16)

    p_e = _bdot(tb, a_e)
    tx = _bdot(tb, x_a.astype(BF16)).astype(BF16)
    q_e = _bdot(tx, jnp.concatenate([zeros_e, v_e], axis=1))
    p_eb = p_e.astype(BF16)
    q_eb = q_e.astype(BF16)
    rp_e = _bdot(x_r, jnp.concatenate([p_eb, zeros_e], axis=1))
    y0_e = _bdot(x_r, jnp.concatenate([q_eb, v_e], axis=1))
    p_sum = head_sum(p_e)
    q_sum = head_sum(q_e)
    rp = (c3d(r_t) + head_sum(rp_e)).astype(BF16)
    y0 = head_sum(y0_e)

    b_3 = bk_3[:, :L, :]
    w_last = c3d(w_in)[:, L - 1:L, :]
    m_lr = (jnp.where(same_head[None], _bdot_tn(p_sum.astype(BF16), b_3), 0.0) * w_last).astype(BF16)
    qv = jnp.concatenate([q_sum, c3d(v)], axis=1).astype(BF16)
    c_bd = jnp.where(same_head[None], _bdot_tn(qv, bk_3), 0.0) * w_last

    s = s_scr[...]
    ys = []
    for c in range(G):
        s_b = s.astype(BF16)
        ys.append(_dot_nt(rp[c], s_b) + y0[c])
        s = s * w_last[c] + _dot(s_b, m_lr[c]) + c_bd[c]
    s_scr[...] = s
    y = jnp.concatenate(ys, axis=0)

    mean = segsum(y) * (1.0 / A_HEAD)
    dlt = y - mean
    var = segsum(dlt * dlt) * (1.0 / A_HEAD)
    yn = dlt * lax.rsqrt(var + A_GN_EPS) * lnw_ref[...] + lnb_ref[...]
    bonus = segsum(r * k * rk_ref[...]) * v
    o_ref[...] = ((yn + bonus) * _dot(zg_ref[...], g2_ref[...])).astype(BF16)


def _rwkv_chunks(rkv, zs, ups, rkv_first, w0, a0, k_k, k_a, r_k, ln_w, ln_b, v0):
    _, t, d = rkv.shape
    n_chunks = 8
    tm = n_chunks * CHUNK
    has_vres = len(zs) == 4
    blk = lambda p, c: (c, p)
    prm = lambda p, c: (0, p)
    w = RWKV_LANES
    slab = lambda j: pl.BlockSpec((1, tm, w), lambda p, c: (j, c, p))
    in_specs = [slab(0), slab(1), slab(2)]
    in_specs += [pl.BlockSpec((tm, z.shape[1]), lambda p, c: (c, 0)) for z in zs]
    if has_vres:
        in_specs += [slab(2)]
    in_specs += [pl.BlockSpec((u.shape[0], w), prm) for u in ups]
    in_specs += [pl.BlockSpec((1, w), prm)] * (8 if has_vres else 7)
    args = [rkv, rkv, rkv] + list(zs)
    if has_vres:
        args += [rkv_first]
    args += list(ups)
    args += [w0, a0, k_k, k_a, r_k, ln_w, ln_b]
    if has_vres:
        args += [v0]
    return pl.pallas_call(
        functools.partial(_rwkv_chunk_kernel, has_vres, n_chunks),
        grid=(d // w, t // tm),
        in_specs=in_specs,
        out_specs=pl.BlockSpec((tm, w), blk),
        out_shape=jax.ShapeDtypeStruct((t, d), BF16),
        scratch_shapes=[pltpu.VMEM((w, w), F32)],
        compiler_params=_cparams(("arbitrary", "arbitrary")),
        name="rwkv_chunks",
    )(*args)


def _dsa_attn_kernel(topk, tk, idx_bits,
                     q_ref, qi_ref, wrow_ref, kidx_ref, ckv_ref, ckvt_ref, wuk_ref, wuv_ref, o_ref,
                     key_scr, qlat_scr, acc_scr, m_scr, l_scr, j0_scr):
    i = pl.program_id(0)
    QB = Q_BLOCK
    cw = ATT_HEADS_PER_STEP * QB
    n_kt = (i * QB + QB + tk - 1) // tk
    qpos = i * QB + lax.broadcasted_iota(I32, (1, QB), 1)
    limit = (qpos // CHUNK + 1) * CHUNK

    def key_pos(off):
        return off + lax.broadcasted_iota(I32, (tk, QB), 0)

    for h in range(B_HEADS):
        ql = _dot(q_ref[0, h], wuk_ref[h]) * (B_SCALE * LOG2_E)
        qlat_scr[h * QB:(h + 1) * QB, :] = ql.astype(BF16)

    qi = qi_ref[0].reshape(B_IDX_HEADS * QB, B_IDX_DIM)
    wrow = wrow_ref[0]

    def score_body(kt, carry):
        off = pl.multiple_of(kt * tk, tk)
        rel = jnp.maximum(_dot_nt(kidx_ref[pl.ds(off, tk), :], qi), 0.0)
        sc = rel[:, 0:QB] * wrow[0:1, :]
        for h in range(1, B_IDX_HEADS):
            sc = sc + rel[:, h * QB:(h + 1) * QB] * wrow[h:h + 1, :]
        sc = jnp.where(sc == 0.0, 0.0, sc)
        bits = pltpu.bitcast(sc, I32)
        key = bits ^ ((bits >> 31) & INT_MAX)
        key_scr[pl.ds(off, tk), :] = jnp.where(key_pos(off) < limit, key, INT_MIN)
        return carry

    lax.fori_loop(0, n_kt, score_body, 0)

    def count_keys(pred):
        def body(kt, acc):
            off = pl.multiple_of(kt * tk, tk)
            hit = jnp.where(pred(key_scr[pl.ds(off, tk), :], off), 1.0, 0.0)
            return acc + jnp.sum(hit.reshape(tk // (8 * COUNT_WAYS), COUNT_WAYS, 8, QB), axis=0)
        acc = lax.fori_loop(0, n_kt, body, jnp.zeros((COUNT_WAYS, 8, QB), F32))
        return jnp.sum(jnp.sum(acc, axis=0), axis=0, keepdims=True)

    def bit_body(b, st):
        thr_u, cnt_ge = st
        cand = thr_u | jnp.left_shift(jnp.int32(1), 31 - b)
        cand_s = cand ^ INT_MIN
        cnt = count_keys(lambda key, off: key >= cand_s)
        take = cnt >= topk
        return jnp.where(take, cand, thr_u), jnp.where(take, cnt, cnt_ge)

    n_scanned = jnp.zeros((1, QB), F32) + (n_kt * tk).astype(F32)
    thr_u, cnt_ge = lax.fori_loop(0, 32, bit_body, (jnp.zeros((1, QB), I32), n_scanned))
    thr = thr_u ^ INT_MIN

    j0_scr[...] = jnp.full((1, QB), INT_MAX, I32)
    excess = jnp.max(jnp.where(thr > INT_MIN, cnt_ge, 0.0)) > topk

    @pl.when(excess)
    def _():
        need = topk - count_keys(lambda key, off: key > thr)

        def tie_body(b, j0):
            cand = j0 | jnp.left_shift(jnp.int32(1), idx_bits - 1 - b)
            cnt = count_keys(lambda key, off: (key == thr) & (key_pos(off) < cand))
            return jnp.where(cnt < need, cand, j0)
        j0_scr[...] = lax.fori_loop(0, idx_bits, tie_body, jnp.zeros((1, QB), I32))

    j0 = j0_scr[...]

    m_scr[...] = jnp.full(m_scr.shape, NEG_BIG, F32)
    l_scr[...] = jnp.zeros_like(l_scr)
    acc_scr[...] = jnp.zeros_like(acc_scr)

    def att_tile(off, size):
        key = key_scr[pl.ds(off, size), :]
        kpos = off + lax.broadcasted_iota(I32, (size, QB), 0)
        sel = ((key > thr) | ((key == thr) & (kpos <= j0))) & (key > INT_MIN)
        bias = jnp.where(sel, 0.0, NEG_BIG)
        bias = jnp.concatenate([bias] * ATT_HEADS_PER_STEP, axis=1)
        ckv_t = ckv_ref[pl.ds(off, size), :]
        ckvt_t = ckvt_ref[:, pl.ds(off, size)]
        n_groups = B_HEADS // ATT_HEADS_PER_STEP

        def logits(c):
            return _dot_nt(ckv_t, qlat_scr[c * cw:(c + 1) * cw, :]) + bias

        s_next = logits(0)
        for c in range(n_groups):
            cols = slice(c * cw, (c + 1) * cw)
            s = s_next
            if c + 1 < n_groups:
                s_next = logits(c + 1)
            m_old = m_scr[:, cols]
            m_new = jnp.maximum(m_old, jnp.max(s, axis=0, keepdims=True))
            alpha = jnp.exp2(m_old - m_new)
            p = jnp.exp2(s - m_new)
            l_scr[:, cols] = alpha * l_scr[:, cols] + jnp.sum(p, axis=0, keepdims=True)
            acc_scr[:, cols] = alpha * acc_scr[:, cols] + _dot(ckvt_t, p.astype(BF16))
            m_scr[:, cols] = m_new

    big = ATT_TILE_MULT * tk

    def att_big(j, carry):
        att_tile(pl.multiple_of(j * big, big), big)
        return carry

    lax.fori_loop(0, n_kt // ATT_TILE_MULT, att_big, 0)
    done = (n_kt // ATT_TILE_MULT) * ATT_TILE_MULT
    mult = ATT_TILE_MULT // 2
    while mult >= 1:
        @pl.when((n_kt & mult) != 0)
        def _(done=done, mult=mult):
            att_tile(pl.multiple_of(done * tk, tk), mult * tk)
        done = done + (n_kt & mult)
        mult //= 2

    o_lat_t = (acc_scr[...] / l_scr[...]).astype(BF16)
    for h in range(B_HEADS):
        oh = _dot_tn(o_lat_t[:, h * QB:(h + 1) * QB], wuv_ref[h])
        o_ref[:, h * B_HEAD_DIM:(h + 1) * B_HEAD_DIM] = oh.astype(BF16)


def _dsa_attn(q, qi, wrow, kidx, ckv, ckvt, w_uk, w_uv):
    nb = q.shape[0]
    t = kidx.shape[0]
    tk = 512
    topk = min(TOPK_MAX, t // 4)
    idx_bits = max(1, (t - 1).bit_length())
    const2 = lambda i: (0, 0)
    const3 = lambda i: (0, 0, 0)
    hq = B_HEADS * Q_BLOCK
    once = pl.Buffered(1)
    return pl.pallas_call(
        functools.partial(_dsa_attn_kernel, topk, tk, idx_bits),
        grid=(nb,),
        in_specs=[
            pl.BlockSpec((1, B_HEADS, Q_BLOCK, B_HEAD_DIM), lambda i: (i, 0, 0, 0)),
            pl.BlockSpec((1, B_IDX_HEADS, Q_BLOCK, B_IDX_DIM), lambda i: (i, 0, 0, 0)),
            pl.BlockSpec((1, B_IDX_HEADS, Q_BLOCK), lambda i: (i, 0, 0)),
            pl.BlockSpec((t, B_IDX_DIM), const2, pipeline_mode=once),
            pl.BlockSpec((t, B_KV_RANK), const2, pipeline_mode=once),
            pl.BlockSpec((B_KV_RANK, t), const2, pipeline_mode=once),
            pl.BlockSpec((B_HEADS, B_HEAD_DIM, B_KV_RANK), const3, pipeline_mode=once),
            pl.BlockSpec((B_HEADS, B_KV_RANK, B_HEAD_DIM), const3, pipeline_mode=once),
        ],
        out_specs=pl.BlockSpec((Q_BLOCK, B_Q_W), lambda i: (i, 0)),
        out_shape=jax.ShapeDtypeStruct((t, B_Q_W), BF16),
        scratch_shapes=[
            pltpu.VMEM((t, Q_BLOCK), I32),
            pltpu.VMEM((hq, B_KV_RANK), BF16),
            pltpu.VMEM((B_KV_RANK, hq), F32),
            pltpu.VMEM((1, hq), F32),
            pltpu.VMEM((1, hq), F32),
            pltpu.VMEM((1, Q_BLOCK), I32),
        ],
        compiler_params=_cparams(("arbitrary",)),
        name="dsa_attn",
    )(q, qi, wrow, kidx, ckv, ckvt, w_uk, w_uv)


def _final_norm_kernel(x_ref, w_ref, o_ref):
    x = x_ref[...]
    o_ref[...] = x * lax.rsqrt(jnp.mean(x * x, axis=-1, keepdims=True) + RMS_EPS) * w_ref[...]


def _final_norm(x, w):
    t, d = x.shape
    tm = 512
    return pl.pallas_call(
        _final_norm_kernel,
        grid=(t // tm,),
        in_specs=[pl.BlockSpec((tm, d), lambda i: (i, 0)), pl.BlockSpec((1, d), lambda i: (0, 0))],
        out_specs=pl.BlockSpec((tm, d), lambda i: (i, 0)),
        out_shape=jax.ShapeDtypeStruct((t, d), F32),
        compiler_params=_cparams(("arbitrary",)),
        name="final_norm",
    )(x, w)


def _row(v):
    return v.reshape(1, -1)


def _rwkv_layer(x, shift, scale, gate, rkv_first, mu, w_rkv, w0, w1, w2, a0, a1, a2,
                g1, g2, k_k, k_a, r_k, ln_w, ln_b, w_o, vres):
    ups = [_pad_rows(w2), _pad_rows(a2), _pad_rows(g2)]
    if vres is None:
        v0, v1 = None, None
    else:
        v0, v1, v2 = vres
        v0, v1 = _row(v0), _pad_cols(v1)
        ups.append(_pad_rows(v2))
    outs = _rwkv_front(x, shift, scale, mu, w_rkv.astype(BF16), _pad_cols(w1), _pad_cols(a1),
                       _pad_cols(g1), v1)
    rkv, zs = outs[0], outs[1:]
    if vres is None:
        rkv_first = rkv
    y = _rwkv_chunks(rkv, zs, ups, rkv_first, _row(w0), _row(a0), _row(k_k), _row(k_a),
                     _row(r_k), _row(ln_w), _row(ln_b), v0)
    return _proj_res(y, w_o.astype(BF16), x, gate), rkv_first


def _dsa_layer(x, shift, scale, gate, w_in, kv_norm, kidx_norm, w_uk, w_uv, w_o):
    q, qi, ckv, ckvt, kidx, wrow = _dsa_front(x, shift, scale, w_in, _row(kv_norm), _row(kidx_norm))
    o = _dsa_attn(q, qi, wrow, kidx, ckv, ckvt, w_uk.astype(BF16), w_uv.astype(BF16))
    return _proj_res(o, w_o.astype(BF16), x, gate)


def kernel(x, c, ada_w, ada_b, a_mu, a_w_rkv, a_w0, a_w1, a_w2, a_a0, a_a1, a_a2, a_v0, a_v1, a_v2, a_g1, a_g2, a_k_k, a_k_a, a_r_k, a_ln_w, a_ln_b, a_w_o, b_w_in, b_kv_norm, b_kidx_norm, b_w_uk, b_w_uv, b_w_o, f_w1, f_w3, f_w2, final_norm):
    b, t, d = x.shape
    assert b == 1 and d == D_MODEL
    mod_all = _ada(c, ada_w, ada_b)
    xs = x.reshape(t, d)
    v_first = None
    for i in range(DEPTH):
        shift1, scale1, gate1, shift2, scale2, gate2 = (
            mod_all[i, :, s * d:(s + 1) * d] for s in range(6))
        j = i // 2
        if i % 2 == 0:
            vres = None if j == 0 else (a_v0[j - 1], a_v1[j - 1], a_v2[j - 1])
            xs, v_first = _rwkv_layer(
                xs, shift1, scale1, gate1, v_first, a_mu[j], a_w_rkv[j], a_w0[j], a_w1[j], a_w2[j],
                a_a0[j], a_a1[j], a_a2[j], a_g1[j], a_g2[j], a_k_k[j], a_k_a[j], a_r_k[j],
                a_ln_w[j], a_ln_b[j], a_w_o[j], vres)
        else:
            xs = _dsa_layer(xs, shift1, scale1, gate1, b_w_in[j], b_kv_norm[j], b_kidx_norm[j],
                            b_w_uk[j], b_w_uv[j], b_w_o[j])
        xs = _ffn(xs, shift2, scale2, gate2, f_w1[i].astype(BF16), f_w3[i].astype(BF16),
                  f_w2[i].astype(BF16))
    return _final_norm(xs, _row(final_norm)).reshape(b, t, d)
```

```python
import functools

import jax
import jax.numpy as jnp
from jax import lax
from jax.experimental import pallas as pl
from jax.experimental.pallas import tpu as pltpu

F32 = jnp.float32
BF16 = jnp.bfloat16
I32 = jnp.int32

D_MODEL = 2048
DEPTH = 4
CHUNK = 64
RMS_EPS = 1e-6
A_HEAD = 64
A_HEADS = D_MODEL // A_HEAD
A_GN_EPS = A_HEAD * 1e-5
B_HEADS = 16
B_HEAD_DIM = 128
B_KV_RANK = 512
B_IDX_HEADS = 8
B_IDX_DIM = 128
TOPK_MAX = 256
Q_BLOCK = 128
B_Q_W = B_HEADS * B_HEAD_DIM
B_QI_W = B_IDX_HEADS * B_IDX_DIM
B_SCALE = B_HEAD_DIM ** -0.5
B_IDX_W_SCALE = (B_IDX_HEADS ** -0.5) * (B_IDX_DIM ** -0.5)
D_FF = 5632

LANES = 128
VMEM_LIMIT = 56 * 1024 * 1024
INT_MIN = -(2 ** 31)
INT_MAX = 2 ** 31 - 1
NEG_BIG = -1e30
LOG2_E = 1.4426950408889634
COUNT_WAYS = 8
ATT_HEADS_PER_STEP = 4
ATT_TILE_MULT = 4
RWKV_LANES = 256

_NT = (((1,), (1,)), ((), ()))
_TN = (((0,), (0,)), ((), ()))


def _cparams(sem):
    return pltpu.CompilerParams(dimension_semantics=sem, vmem_limit_bytes=VMEM_LIMIT)


def _dot(a, b):
    return jnp.dot(a, b, preferred_element_type=F32)


def _dot_nt(a, b):
    return lax.dot_general(a, b, _NT, preferred_element_type=F32)


def _dot_tn(a, b):
    return lax.dot_general(a, b, _TN, preferred_element_type=F32)


def _modulate(x, shift, scale):
    ms = jnp.mean(x * x, axis=-1, keepdims=True)
    return x * lax.rsqrt(ms + RMS_EPS) * (1.0 + scale) + shift


def _sigmoid(x):
    return 1.0 / (1.0 + jnp.exp(-x))


def _ada_kernel(c_ref, w_ref, b_ref, o_ref):
    c = c_ref[...]
    s = c * _sigmoid(c)
    o_ref[0] = jnp.sum(s * w_ref[0], axis=0, keepdims=True) + b_ref[0]


def _ada(c, ada_w, ada_b):
    depth, d, n = ada_w.shape
    tn = 1024
    return pl.pallas_call(
        _ada_kernel,
        grid=(depth, n // tn),
        in_specs=[
            pl.BlockSpec((d, 1), lambda l, j: (0, 0)),
            pl.BlockSpec((1, d, tn), lambda l, j: (l, 0, j)),
            pl.BlockSpec((1, 1, tn), lambda l, j: (l, 0, j)),
        ],
        out_specs=pl.BlockSpec((1, 1, tn), lambda l, j: (l, 0, j)),
        out_shape=jax.ShapeDtypeStruct((depth, 1, n), F32),
        compiler_params=_cparams(("arbitrary", "arbitrary")),
        name="ada_mod",
    )(c.reshape(d, 1), ada_w, ada_b.reshape(depth, 1, n))


def _ffn_kernel(x_ref, sh_ref, sc_ref, g_ref, w1_ref, w3_ref, w2_ref, o_ref, h_scr, acc_scr):
    f = pl.program_id(1)

    @pl.when(f == 0)
    def _():
        h_scr[...] = _modulate(x_ref[...], sh_ref[...], sc_ref[...]).astype(BF16)
        acc_scr[...] = jnp.zeros_like(acc_scr)

    h = h_scr[...]
    a = _dot(h, w1_ref[...])
    b = _dot(h, w3_ref[...])
    u = (a * _sigmoid(a) * b).astype(BF16)
    acc_scr[...] += _dot(u, w2_ref[...])

    @pl.when(f == pl.num_programs(1) - 1)
    def _():
        o_ref[...] = x_ref[...] + g_ref[...] * acc_scr[...]


def _ffn(x, shift, scale, gate, w1, w3, w2, layer):
    t, d = x.shape
    dff = w1.shape[2]
    tm, tf = 512, 512
    row = lambda i, f: (0, 0)
    return pl.pallas_call(
        _ffn_kernel,
        grid=(t // tm, dff // tf),
        in_specs=[
            pl.BlockSpec((tm, d), lambda i, f: (i, 0)),
            pl.BlockSpec((1, d), row),
            pl.BlockSpec((1, d), row),
            pl.BlockSpec((1, d), row),
            pl.BlockSpec((None, d, tf), lambda i, f: (layer, 0, f)),
            pl.BlockSpec((None, d, tf), lambda i, f: (layer, 0, f)),
            pl.BlockSpec((None, tf, d), lambda i, f: (layer, f, 0)),
        ],
        out_specs=pl.BlockSpec((tm, d), lambda i, f: (i, 0)),
        out_shape=jax.ShapeDtypeStruct((t, d), F32),
        scratch_shapes=[pltpu.VMEM((tm, d), BF16), pltpu.VMEM((tm, d), F32)],
        compiler_params=_cparams(("arbitrary", "arbitrary")),
        name="ffn",
    )(x, shift, scale, gate, w1, w3, w2)


def _proj_res_kernel(a_ref, w_ref, x_ref, g_ref, o_ref):
    o_ref[...] = x_ref[...] + g_ref[...] * _dot(a_ref[...], w_ref[...])


def _proj_res(a, w, layer, x, gate):
    t, k = a.shape
    n = w.shape[2]
    tm, tn = 512, n
    return pl.pallas_call(
        _proj_res_kernel,
        grid=(t // tm, n // tn),
        in_specs=[
            pl.BlockSpec((tm, k), lambda i, j: (i, 0)),
            pl.BlockSpec((None, k, tn), lambda i, j: (layer, 0, j), pipeline_mode=pl.Buffered(1)),
            pl.BlockSpec((tm, tn), lambda i, j: (i, j)),
            pl.BlockSpec((1, tn), lambda i, j: (0, j)),
        ],
        out_specs=pl.BlockSpec((tm, tn), lambda i, j: (i, j)),
        out_shape=jax.ShapeDtypeStruct((t, n), F32),
        compiler_params=_cparams(("arbitrary", "arbitrary")),
        name="proj_res",
    )(a, w, x, gate)


DSA_TN = 512
DSA_HEADS_PER_TILE = DSA_TN // B_HEAD_DIM
DSA_Q_TILES = B_Q_W // DSA_TN
DSA_KV_TILE = DSA_Q_TILES
DSA_QI_TILE0 = DSA_KV_TILE + B_KV_RANK // DSA_TN
DSA_QI_TILES = B_QI_W // DSA_TN
DSA_KI_TILE = DSA_QI_TILE0 + DSA_QI_TILES


def _dsa_front_kernel(x_ref, sh_ref, sc_ref, w_ref, kvn_ref, kin_ref,
                      q_ref, qi_ref, ckv_ref, ckvt_ref, kidx_ref, wrow_ref, h_scr):
    j = pl.program_id(1)
    nbk = x_ref.shape[0] // Q_BLOCK

    @pl.when(j == 0)
    def _():
        h_scr[...] = _modulate(x_ref[...], sh_ref[...], sc_ref[...]).astype(BF16)

    y = _dot(h_scr[...], w_ref[...])

    def put_heads(ref):
        for b in range(nbk):
            for h in range(DSA_HEADS_PER_TILE):
                ref[b, h] = y[b * Q_BLOCK:(b + 1) * Q_BLOCK,
                              h * B_HEAD_DIM:(h + 1) * B_HEAD_DIM].astype(BF16)

    @pl.when(j < DSA_Q_TILES)
    def _():
        put_heads(q_ref)

    @pl.when(j == DSA_KV_TILE)
    def _():
        ckv = y * lax.rsqrt(jnp.mean(y * y, axis=-1, keepdims=True) + RMS_EPS) * kvn_ref[...]
        ckv_ref[...] = ckv.astype(BF16)
        ckvt_ref[...] = ckv.T.astype(BF16)

    @pl.when((j >= DSA_QI_TILE0) & (j < DSA_KI_TILE))
    def _():
        put_heads(qi_ref)

    @pl.when(j == DSA_KI_TILE)
    def _():
        kidx = y[:, :B_IDX_DIM]
        kidx = kidx * lax.rsqrt(jnp.mean(kidx * kidx, axis=-1, keepdims=True) + RMS_EPS) * kin_ref[...]
        kidx_ref[...] = kidx.astype(BF16)
        wi = y[:, B_IDX_DIM:B_IDX_DIM + LANES] * B_IDX_W_SCALE
        for b in range(nbk):
            wrow_ref[b] = wi[b * Q_BLOCK:(b + 1) * Q_BLOCK, :].T[:B_IDX_HEADS, :]


def _dsa_front(x, shift, scale, w_in, kv_norm, kidx_norm):
    t, d = x.shape
    assert B_HEAD_DIM == B_IDX_DIM and B_KV_RANK == DSA_TN and w_in.shape[1] <= (DSA_KI_TILE + 1) * DSA_TN
    n_tiles = DSA_KI_TILE + 1
    w = jnp.pad(w_in, ((0, 0), (0, n_tiles * DSA_TN - w_in.shape[1]))).astype(BF16)
    tm = 512
    nbk = tm // Q_BLOCK
    nb = t // Q_BLOCK
    row = lambda i, j: (0, 0)
    hpt = DSA_HEADS_PER_TILE
    return pl.pallas_call(
        _dsa_front_kernel,
        grid=(t // tm, n_tiles),
        in_specs=[
            pl.BlockSpec((tm, d), lambda i, j: (i, 0)),
            pl.BlockSpec((1, d), row),
            pl.BlockSpec((1, d), row),
            pl.BlockSpec((d, DSA_TN), lambda i, j: (0, j)),
            pl.BlockSpec((1, B_KV_RANK), row),
            pl.BlockSpec((1, B_IDX_DIM), row),
        ],
        out_specs=[
            pl.BlockSpec((nbk, hpt, Q_BLOCK, B_HEAD_DIM),
                         lambda i, j: (i, jnp.minimum(j, DSA_Q_TILES - 1), 0, 0)),
            pl.BlockSpec((nbk, hpt, Q_BLOCK, B_IDX_DIM),
                         lambda i, j: (i, jnp.clip(j - DSA_QI_TILE0, 0, DSA_QI_TILES - 1), 0, 0)),
            pl.BlockSpec((tm, B_KV_RANK), lambda i, j: (i, 0)),
            pl.BlockSpec((B_KV_RANK, tm), lambda i, j: (0, i)),
            pl.BlockSpec((tm, B_IDX_DIM), lambda i, j: (i, 0)),
            pl.BlockSpec((nbk, B_IDX_HEADS, Q_BLOCK), lambda i, j: (i, 0, 0)),
        ],
        out_shape=[
            jax.ShapeDtypeStruct((nb, B_HEADS, Q_BLOCK, B_HEAD_DIM), BF16),
            jax.ShapeDtypeStruct((nb, B_IDX_HEADS, Q_BLOCK, B_IDX_DIM), BF16),
            jax.ShapeDtypeStruct((t, B_KV_RANK), BF16),
            jax.ShapeDtypeStruct((B_KV_RANK, t), BF16),
            jax.ShapeDtypeStruct((t, B_IDX_DIM), BF16),
            jax.ShapeDtypeStruct((nb, B_IDX_HEADS, Q_BLOCK), F32),
        ],
        scratch_shapes=[pltpu.VMEM((tm, d), BF16)],
        compiler_params=_cparams(("arbitrary", "arbitrary")),
        name="dsa_front",
    )(x, shift, scale, w, kv_norm, kidx_norm)


def _rwkv_front_kernel(has_vres, x_ref, sh_ref, sc_ref, mu_ref, wrkv_ref, w1_ref, a1_ref, g1_ref,
                       *rest):
    if has_vres:
        v1_ref, rkv_ref, zw_ref, za_ref, zg_ref, zv_ref, carry = rest
    else:
        rkv_ref, zw_ref, za_ref, zg_ref, carry = rest
    i = pl.program_id(0)
    tm = x_ref.shape[0]

    @pl.when(i == 0)
    def _():
        carry[...] = jnp.zeros_like(carry)

    h = _modulate(x_ref[...], sh_ref[...], sc_ref[...])
    prev_last = carry[7:8, :]
    rolled = pltpu.roll(h, 1, axis=0)
    rows = lax.broadcasted_iota(I32, h.shape, 0)
    h_prev = jnp.where(rows == 0, prev_last, rolled)
    carry[...] = h[tm - 8:tm, :]
    dx = h_prev - h

    def mix(j):
        return (h + dx * mu_ref[j:j + 1, :]).astype(BF16)

    xv = mix(3)
    rkv_ref[0] = _dot(mix(0), wrkv_ref[0])
    rkv_ref[1] = _dot(mix(2), wrkv_ref[1])
    rkv_ref[2] = _dot(xv, wrkv_ref[2])
    zw_ref[...] = jnp.tanh(_dot(mix(1), w1_ref[...])).astype(BF16)
    za_ref[...] = _dot(mix(4), a1_ref[...]).astype(BF16)
    zg_ref[...] = _sigmoid(_dot(mix(5), g1_ref[...])).astype(BF16)
    if has_vres:
        zv_ref[...] = _dot(xv, v1_ref[...]).astype(BF16)


def _pad_cols(w):
    r = w.shape[1]
    return jnp.pad(w, ((0, 0), (0, -(-r // LANES) * LANES - r))).astype(BF16)


def _pad_rows(w):
    r = w.shape[0]
    return jnp.pad(w, ((0, -(-r // LANES) * LANES - r), (0, 0))).astype(BF16)


def _rwkv_front(x, shift, scale, mu, w_rkv, layer, w1, a1, g1, v1):
    t, d = x.shape
    tm = 256
    has_vres = v1 is not None
    once = pl.Buffered(1)
    lora_ws = [w1, a1, g1] + ([v1] if has_vres else [])
    const2 = lambda i: (0, 0)
    in_specs = [
        pl.BlockSpec((tm, d), lambda i: (i, 0)),
        pl.BlockSpec((1, d), const2),
        pl.BlockSpec((1, d), const2),
        pl.BlockSpec((6, d), const2),
        pl.BlockSpec((None, 3, d, d), lambda i: (layer, 0, 0, 0), pipeline_mode=once),
    ] + [pl.BlockSpec(w.shape, const2, pipeline_mode=once) for w in lora_ws]
    out_specs = [pl.BlockSpec((3, tm, d), lambda i: (0, i, 0))]
    out_specs += [pl.BlockSpec((tm, w.shape[1]), lambda i: (i, 0)) for w in lora_ws]
    out_shape = [jax.ShapeDtypeStruct((3, t, d), F32)]
    out_shape += [jax.ShapeDtypeStruct((t, w.shape[1]), BF16) for w in lora_ws]
    return pl.pallas_call(
        functools.partial(_rwkv_front_kernel, has_vres),
        grid=(t // tm,),
        in_specs=in_specs,
        out_specs=out_specs,
        out_shape=out_shape,
        scratch_shapes=[pltpu.VMEM((8, d), F32)],
        compiler_params=_cparams(("arbitrary",)),
        name="rwkv_front",
    )(x, shift, scale, mu, w_rkv, *lora_ws)


def _split2(x):
    hi = x.astype(BF16)
    lo = (x - hi.astype(F32)).astype(BF16)
    return hi, lo


def _bdot(a, b):
    return lax.dot_general(a, b, (((2,), (1,)), ((0,), (0,))), preferred_element_type=F32)


def _bdot_nt(a, b):
    return lax.dot_general(a, b, (((2,), (2,)), ((0,), (0,))), preferred_element_type=F32)


def _bdot_tn(a, b):
    return lax.dot_general(a, b, (((1,), (1,)), ((0,), (0,))), preferred_element_type=F32)


def _rwkv_chunk_kernel(has_vres, n_chunks, *refs):
    if has_vres:
        (r_ref, k_ref, v_ref, zw_ref, za_ref, zg_ref, zv_ref, vf_ref,
         w2_ref, a2_ref, g2_ref, v2_ref,
         w0_ref, a0_ref, kk_ref, ka_ref, rk_ref, lnw_ref, lnb_ref, v0_ref,
         o_ref, s_scr) = refs
    else:
        (r_ref, k_ref, v_ref, zw_ref, za_ref, zg_ref,
         w2_ref, a2_ref, g2_ref,
         w0_ref, a0_ref, kk_ref, ka_ref, rk_ref, lnw_ref, lnb_ref,
         o_ref, s_scr) = refs

    L = CHUNK
    W = o_ref.shape[1]
    NH = W // A_HEAD
    G = n_chunks
    R = G * L

    @pl.when(pl.program_id(1) == 0)
    def _():
        s_scr[...] = jnp.zeros_like(s_scr)

    lane = lax.broadcasted_iota(I32, (1, W), 1)
    head_masks = [((lane // A_HEAD) == e).astype(F32) for e in range(NH)]
    rr = lax.broadcasted_iota(I32, (L, L), 0)
    cc = lax.broadcasted_iota(I32, (L, L), 1)
    rr2 = lax.broadcasted_iota(I32, (L, 2 * L), 0)
    cc2 = lax.broadcasted_iota(I32, (L, 2 * L), 1) % L
    low_strict2 = (rr2 > cc2)[None]
    low_incl2 = (rr2 >= cc2)[None]
    eye_l = (rr == cc).astype(F32)[None]
    diag_blk = ((rr >> 3) == (cc >> 3))[None]
    merge_blks = [(((rr >> (s + 1)) == (cc >> (s + 1))) & ((rr >> s) > (cc >> s)))[None]
                  for s in (3, 4, 5)]
    r2 = lax.broadcasted_iota(I32, (W, W), 0)
    c2 = lax.broadcasted_iota(I32, (W, W), 1)
    same_head = (r2 // A_HEAD) == (c2 // A_HEAD)
    seg_ones = same_head.astype(BF16)
    r3 = lax.broadcasted_iota(I32, (R, R), 0)
    c3 = lax.broadcasted_iota(I32, (R, R), 1)
    tri_ones = (((r3 // L) == (c3 // L)) & (r3 >= c3)).astype(BF16)

    def segsum(x):
        hi, lo = _split2(x)
        return _dot(hi, seg_ones) + _dot(lo, seg_ones)

    r = r_ref[0]
    k = k_ref[0]
    v = v_ref[0]
    z = -(w0_ref[...] + _dot(zw_ref[...], w2_ref[...]))
    softplus = jnp.maximum(z, 0.0) + jnp.log(1.0 + jnp.exp(-jnp.abs(z)))
    logw = -jnp.exp(-softplus - 0.5)
    a = _sigmoid(a0_ref[...] + _dot(za_ref[...], a2_ref[...]))
    if has_vres:
        v = v + (vf_ref[0] - v) * _sigmoid(v0_ref[...] + _dot(zv_ref[...], v2_ref[...]))
    kk = k * kk_ref[...]
    kk = kk / jnp.maximum(jnp.sqrt(segsum(kk * kk)), 1e-12)
    k = k * (1.0 + (a - 1.0) * ka_ref[...])

    h1, h2 = _split2(logw)
    cum = _dot(tri_ones, h1) + _dot(tri_ones, h2)
    w_in = jnp.exp(cum)
    w_inv = jnp.exp(-cum)
    a_t = -kk * jnp.exp(cum - logw)
    b_t = kk * a * w_inv
    k_t = k * w_inv
    r_t = r * w_in

    def c3d(x):
        return x.reshape(G, L, W)

    def per_head(x):
        x3 = c3d(x)
        return jnp.concatenate([x3 * hm for hm in head_masks], axis=0).astype(BF16)

    def head_sum(x):
        out = x[:G]
        for e in range(1, NH):
            out = out + x[e * G:(e + 1) * G]
        return out

    a_e = per_head(a_t)
    r_e = per_head(r_t)
    v_e = per_head(v)
    bk_3 = jnp.concatenate([c3d(b_t), c3d(k_t)], axis=1).astype(BF16)
    bk_e = jnp.concatenate([bk_3] * NH, axis=0)
    zeros_e = jnp.zeros_like(v_e)

    x_a = jnp.where(low_strict2, _bdot_nt(a_e, bk_e), 0.0)
    x_r = jnp.where(low_incl2, _bdot_nt(r_e, bk_e), 0.0).astype(BF16)
    x_ab = x_a[:, :, :L]

    xp = jnp.where(diag_blk, x_ab, 0.0)
    tm = eye_l + xp
    for _ in range(2):
        xb = xp.astype(BF16)
        xp = _bdot(xb, xb)
        tm = tm + _bdot(tm.astype(BF16), xp.astype(BF16))
    for off_blk in merge_blks:
        tb = tm.astype(BF16)
        x_off = jnp.where(off_blk, x_ab, 0.0).astype(BF16)
        tm = tm + _bdot(_bdot(tb, x_off).astype(BF16), tb)
    tb = tm.astype(BF16)

    p_e = _bdot(tb, a_e)
    tx = _bdot(tb, x_a.astype(BF16)).astype(BF16)
    q_e = _bdot(tx, jnp.concatenate([zeros_e, v_e], axis=1))
    p_eb = p_e.astype(BF16)
    q_eb = q_e.astype(BF16)
    rp_e = _bdot(x_r, jnp.concatenate([p_eb, zeros_e], axis=1))
    y0_e = _bdot(x_r, jnp.concatenate([q_eb, v_e], axis=1))
    p_sum = head_sum(p_e)
    q_sum = head_sum(q_e)
    rp = (c3d(r_t) + head_sum(rp_e)).astype(BF16)
    y0 = head_sum(y0_e)

    b_3 = bk_3[:, :L, :]
    w_last = c3d(w_in)[:, L - 1:L, :]
    m_lr = (jnp.where(same_head[None], _bdot_tn(p_sum.astype(BF16), b_3), 0.0) * w_last).astype(BF16)
    qv = jnp.concatenate([q_sum, c3d(v)], axis=1).astype(BF16)
    c_bd = jnp.where(same_head[None], _bdot_tn(qv, bk_3), 0.0) * w_last

    s = s_scr[...]
    ys = []
    for c in range(G):
        s_b = s.astype(BF16)
        ys.append(_dot_nt(rp[c], s_b) + y0[c])
        s = s * w_last[c] + _dot(s_b, m_lr[c]) + c_bd[c]
    s_scr[...] = s
    y = jnp.concatenate(ys, axis=0)

    mean = segsum(y) * (1.0 / A_HEAD)
    dlt = y - mean
    var = segsum(dlt * dlt) * (1.0 / A_HEAD)
    yn = dlt * lax.rsqrt(var + A_GN_EPS) * lnw_ref[...] + lnb_ref[...]
    bonus = segsum(r * k * rk_ref[...]) * v
    o_ref[...] = ((yn + bonus) * _dot(zg_ref[...], g2_ref[...])).astype(BF16)


def _rwkv_chunks(rkv, zs, ups, rkv_first, w0, a0, k_k, k_a, r_k, ln_w, ln_b, v0):
    _, t, d = rkv.shape
    n_chunks = 8
    tm = n_chunks * CHUNK
    has_vres = len(zs) == 4
    blk = lambda p, c: (c, p)
    prm = lambda p, c: (0, p)
    w = RWKV_LANES
    slab = lambda j: pl.BlockSpec((1, tm, w), lambda p, c: (j, c, p))
    in_specs = [slab(0), slab(1), slab(2)]
    in_specs += [pl.BlockSpec((tm, z.shape[1]), lambda p, c: (c, 0)) for z in zs]
    if has_vres:
        in_specs += [slab(2)]
    in_specs += [pl.BlockSpec((u.shape[0], w), prm) for u in ups]
    in_specs += [pl.BlockSpec((1, w), prm)] * (8 if has_vres else 7)
    args = [rkv, rkv, rkv] + list(zs)
    if has_vres:
        args += [rkv_first]
    args += list(ups)
    args += [w0, a0, k_k, k_a, r_k, ln_w, ln_b]
    if has_vres:
        args += [v0]
    return pl.pallas_call(
        functools.partial(_rwkv_chunk_kernel, has_vres, n_chunks),
        grid=(d // w, t // tm),
        in_specs=in_specs,
        out_specs=pl.BlockSpec((tm, w), blk),
        out_shape=jax.ShapeDtypeStruct((t, d), BF16),
        scratch_shapes=[pltpu.VMEM((w, w), F32)],
        compiler_params=_cparams(("arbitrary", "arbitrary")),
        name="rwkv_chunks",
    )(*args)


def _dsa_attn_kernel(topk, tk, idx_bits,
                     q_ref, qi_ref, wrow_ref, kidx_ref, ckv_ref, ckvt_ref, wuk_ref, wuv_ref, o_ref,
                     key_scr, qlat_scr, acc_scr, m_scr, l_scr, j0_scr):
    i = pl.program_id(0)
    QB = Q_BLOCK
    cw = ATT_HEADS_PER_STEP * QB
    n_kt = (i * QB + QB + tk - 1) // tk
    qpos = i * QB + lax.broadcasted_iota(I32, (1, QB), 1)
    limit = (qpos // CHUNK + 1) * CHUNK

    def key_pos(off):
        return off + lax.broadcasted_iota(I32, (tk, QB), 0)

    for h in range(B_HEADS):
        ql = _dot(q_ref[0, h], wuk_ref[h]) * (B_SCALE * LOG2_E)
        qlat_scr[h * QB:(h + 1) * QB, :] = ql.astype(BF16)

    qi = qi_ref[0].reshape(B_IDX_HEADS * QB, B_IDX_DIM)
    wrow = wrow_ref[0]

    def score_body(kt, carry):
        off = pl.multiple_of(kt * tk, tk)
        rel = jnp.maximum(_dot_nt(kidx_ref[pl.ds(off, tk), :], qi), 0.0)
        sc = rel[:, 0:QB] * wrow[0:1, :]
        for h in range(1, B_IDX_HEADS):
            sc = sc + rel[:, h * QB:(h + 1) * QB] * wrow[h:h + 1, :]
        sc = jnp.where(sc == 0.0, 0.0, sc)
        bits = pltpu.bitcast(sc, I32)
        key = bits ^ ((bits >> 31) & INT_MAX)
        key_scr[pl.ds(off, tk), :] = jnp.where(key_pos(off) < limit, key, INT_MIN)
        return carry

    lax.fori_loop(0, n_kt, score_body, 0)

    def count_keys(pred):
        def body(kt, acc):
            off = pl.multiple_of(kt * tk, tk)
            hit = jnp.where(pred(key_scr[pl.ds(off, tk), :], off), 1.0, 0.0)
            return acc + jnp.sum(hit.reshape(tk // (8 * COUNT_WAYS), COUNT_WAYS, 8, QB), axis=0)
        acc = lax.fori_loop(0, n_kt, body, jnp.zeros((COUNT_WAYS, 8, QB), F32))
        return jnp.sum(jnp.sum(acc, axis=0), axis=0, keepdims=True)

    def bit_body(b, st):
        thr_u, cnt_ge = st
        cand = thr_u | jnp.left_shift(jnp.int32(1), 31 - b)
        cand_s = cand ^ INT_MIN
        cnt = count_keys(lambda key, off: key >= cand_s)
        take = cnt >= topk
        return jnp.where(take, cand, thr_u), jnp.where(take, cnt, cnt_ge)

    n_scanned = jnp.zeros((1, QB), F32) + (n_kt * tk).astype(F32)
    thr_u, cnt_ge = lax.fori_loop(0, 32, bit_body, (jnp.zeros((1, QB), I32), n_scanned))
    thr = thr_u ^ INT_MIN

    j0_scr[...] = jnp.full((1, QB), INT_MAX, I32)
    excess = jnp.max(jnp.where(thr > INT_MIN, cnt_ge, 0.0)) > topk

    @pl.when(excess)
    def _():
        need = topk - count_keys(lambda key, off: key > thr)

        def tie_body(b, j0):
            cand = j0 | jnp.left_shift(jnp.int32(1), idx_bits - 1 - b)
            cnt = count_keys(lambda key, off: (key == thr) & (key_pos(off) < cand))
            return jnp.where(cnt < need, cand, j0)
        j0_scr[...] = lax.fori_loop(0, idx_bits, tie_body, jnp.zeros((1, QB), I32))

    j0 = j0_scr[...]

    m_scr[...] = jnp.full(m_scr.shape, NEG_BIG, F32)
    l_scr[...] = jnp.zeros_like(l_scr)
    acc_scr[...] = jnp.zeros_like(acc_scr)

    def att_tile(off, size):
        key = key_scr[pl.ds(off, size), :]
        kpos = off + lax.broadcasted_iota(I32, (size, QB), 0)
        sel = ((key > thr) | ((key == thr) & (kpos <= j0))) & (key > INT_MIN)
        bias = jnp.where(sel, 0.0, NEG_BIG)
        bias = jnp.concatenate([bias] * ATT_HEADS_PER_STEP, axis=1)
        ckv_t = ckv_ref[pl.ds(off, size), :]
        ckvt_t = ckvt_ref[:, pl.ds(off, size)]
        n_groups = B_HEADS // ATT_HEADS_PER_STEP

        def logits(c):
            return _dot_nt(ckv_t, qlat_scr[c * cw:(c + 1) * cw, :]) + bias

        s_next = logits(0)
        for c in range(n_groups):
            cols = slice(c * cw, (c + 1) * cw)
            s = s_next
            if c + 1 < n_groups:
                s_next = logits(c + 1)
            m_old = m_scr[:, cols]
            m_new = jnp.maximum(m_old, jnp.max(s, axis=0, keepdims=True))
            alpha = jnp.exp2(m_old - m_new)
            p = jnp.exp2(s - m_new)
            l_scr[:, cols] = alpha * l_scr[:, cols] + jnp.sum(p, axis=0, keepdims=True)
            acc_scr[:, cols] = alpha * acc_scr[:, cols] + _dot(ckvt_t, p.astype(BF16))
            m_scr[:, cols] = m_new

    big = ATT_TILE_MULT * tk

    def att_big(j, carry):
        att_tile(pl.multiple_of(j * big, big), big)
        return carry

    lax.fori_loop(0, n_kt // ATT_TILE_MULT, att_big, 0)
    done = (n_kt // ATT_TILE_MULT) * ATT_TILE_MULT
    mult = ATT_TILE_MULT // 2
    while mult >= 1:
        @pl.when((n_kt & mult) != 0)
        def _(done=done, mult=mult):
            att_tile(pl.multiple_of(done * tk, tk), mult * tk)
        done = done + (n_kt & mult)
        mult //= 2

    o_lat_t = (acc_scr[...] / l_scr[...]).astype(BF16)
    for h in range(B_HEADS):
        oh = _dot_tn(o_lat_t[:, h * QB:(h + 1) * QB], wuv_ref[h])
        o_ref[:, h * B_HEAD_DIM:(h + 1) * B_HEAD_DIM] = oh.astype(BF16)


def _dsa_attn(q, qi, wrow, kidx, ckv, ckvt, w_uk, w_uv):
    nb = q.shape[0]
    t = kidx.shape[0]
    tk = 512
    topk = min(TOPK_MAX, t // 4)
    idx_bits = max(1, (t - 1).bit_length())
    const2 = lambda i: (0, 0)
    const3 = lambda i: (0, 0, 0)
    hq = B_HEADS * Q_BLOCK
    once = pl.Buffered(1)
    return pl.pallas_call(
        functools.partial(_dsa_attn_kernel, topk, tk, idx_bits),
        grid=(nb,),
        in_specs=[
            pl.BlockSpec((1, B_HEADS, Q_BLOCK, B_HEAD_DIM), lambda i: (i, 0, 0, 0)),
            pl.BlockSpec((1, B_IDX_HEADS, Q_BLOCK, B_IDX_DIM), lambda i: (i, 0, 0, 0)),
            pl.BlockSpec((1, B_IDX_HEADS, Q_BLOCK), lambda i: (i, 0, 0)),
            pl.BlockSpec((t, B_IDX_DIM), const2, pipeline_mode=once),
            pl.BlockSpec((t, B_KV_RANK), const2, pipeline_mode=once),
            pl.BlockSpec((B_KV_RANK, t), const2, pipeline_mode=once),
            pl.BlockSpec((B_HEADS, B_HEAD_DIM, B_KV_RANK), const3, pipeline_mode=once),
            pl.BlockSpec((B_HEADS, B_KV_RANK, B_HEAD_DIM), const3, pipeline_mode=once),
        ],
        out_specs=pl.BlockSpec((Q_BLOCK, B_Q_W), lambda i: (i, 0)),
        out_shape=jax.ShapeDtypeStruct((t, B_Q_W), BF16),
        scratch_shapes=[
            pltpu.VMEM((t, Q_BLOCK), I32),
            pltpu.VMEM((hq, B_KV_RANK), BF16),
            pltpu.VMEM((B_KV_RANK, hq), F32),
            pltpu.VMEM((1, hq), F32),
            pltpu.VMEM((1, hq), F32),
            pltpu.VMEM((1, Q_BLOCK), I32),
        ],
        compiler_params=_cparams(("arbitrary",)),
        name="dsa_attn",
    )(q, qi, wrow, kidx, ckv, ckvt, w_uk, w_uv)


def _final_norm_kernel(x_ref, w_ref, o_ref):
    x = x_ref[...]
    o_ref[...] = x * lax.rsqrt(jnp.mean(x * x, axis=-1, keepdims=True) + RMS_EPS) * w_ref[...]


def _final_norm(x, w):
    t, d = x.shape
    tm = 512
    return pl.pallas_call(
        _final_norm_kernel,
        grid=(t // tm,),
        in_specs=[pl.BlockSpec((tm, d), lambda i: (i, 0)), pl.BlockSpec((1, d), lambda i: (0, 0))],
        out_specs=pl.BlockSpec((tm, d), lambda i: (i, 0)),
        out_shape=jax.ShapeDtypeStruct((t, d), F32),
        compiler_params=_cparams(("arbitrary",)),
        name="final_norm",
    )(x, w)


def _row(v):
    return v.reshape(1, -1)


def _rwkv_layer(x, shift, scale, gate, rkv_first, layer, mu, w_rkv, w0, w1, w2, a0, a1, a2,
                g1, g2, k_k, k_a, r_k, ln_w, ln_b, w_o, vres):
    ups = [_pad_rows(w2), _pad_rows(a2), _pad_rows(g2)]
    if vres is None:
        v0, v1 = None, None
    else:
        v0, v1, v2 = vres
        v0, v1 = _row(v0), _pad_cols(v1)
        ups.append(_pad_rows(v2))
    outs = _rwkv_front(x, shift, scale, mu, w_rkv, layer, _pad_cols(w1), _pad_cols(a1),
                       _pad_cols(g1), v1)
    rkv, zs = outs[0], outs[1:]
    if vres is None:
        rkv_first = rkv
    y = _rwkv_chunks(rkv, zs, ups, rkv_first, _row(w0), _row(a0), _row(k_k), _row(k_a),
                     _row(r_k), _row(ln_w), _row(ln_b), v0)
    return _proj_res(y, w_o, layer, x, gate), rkv_first


def _dsa_layer(x, shift, scale, gate, layer, w_in, kv_norm, kidx_norm, w_uk, w_uv, w_o):
    q, qi, ckv, ckvt, kidx, wrow = _dsa_front(x, shift, scale, w_in, _row(kv_norm), _row(kidx_norm))
    o = _dsa_attn(q, qi, wrow, kidx, ckv, ckvt, w_uk.astype(BF16), w_uv.astype(BF16))
    return _proj_res(o, w_o, layer, x, gate)


def kernel(x, c, ada_w, ada_b, a_mu, a_w_rkv, a_w0, a_w1, a_w2, a_a0, a_a1, a_a2, a_v0, a_v1, a_v2, a_g1, a_g2, a_k_k, a_k_a, a_r_k, a_ln_w, a_ln_b, a_w_o, b_w_in, b_kv_norm, b_kidx_norm, b_w_uk, b_w_uv, b_w_o, f_w1, f_w3, f_w2, final_norm):
    b, t, d = x.shape
    assert b == 1 and d == D_MODEL
    mod_all = _ada(c, ada_w, ada_b)
    xs = x.reshape(t, d)
    v_first = None
    a_w_rkv, a_w_o, b_w_o = a_w_rkv.astype(BF16), a_w_o.astype(BF16), b_w_o.astype(BF16)
    f_w1, f_w3, f_w2 = f_w1.astype(BF16), f_w3.astype(BF16), f_w2.astype(BF16)
    for i in range(DEPTH):
        shift1, scale1, gate1, shift2, scale2, gate2 = (
            mod_all[i, :, s * d:(s + 1) * d] for s in range(6))
        j = i // 2
        if i % 2 == 0:
            vres = None if j == 0 else (a_v0[j - 1], a_v1[j - 1], a_v2[j - 1])
            xs, v_first = _rwkv_layer(
                xs, shift1, scale1, gate1, v_first, j, a_mu[j], a_w_rkv, a_w0[j], a_w1[j], a_w2[j],
                a_a0[j], a_a1[j], a_a2[j], a_g1[j], a_g2[j], a_k_k[j], a_k_a[j], a_r_k[j],
                a_ln_w[j], a_ln_b[j], a_w_o, vres)
        else:
            xs = _dsa_layer(xs, shift1, scale1, gate1, j, b_w_in[j], b_kv_norm[j], b_kidx_norm[j],
                            b_w_uk[j], b_w_uv[j], b_w_o)
        xs = _ffn(xs, shift2, scale2, gate2, f_w1, f_w3, f_w2, i)
    return _final_norm(xs, _row(final_norm)).reshape(b, t, d)
```

```python
import functools

import jax
import jax.numpy as jnp
from jax import lax
from jax.experimental import pallas as pl
from jax.experimental.pallas import tpu as pltpu

F32 = jnp.float32
BF16 = jnp.bfloat16
I32 = jnp.int32

D_MODEL = 2048
DEPTH = 4
CHUNK = 64
RMS_EPS = 1e-6
A_HEAD = 64
A_HEADS = D_MODEL // A_HEAD
A_GN_EPS = A_HEAD * 1e-5
B_HEADS = 16
B_HEAD_DIM = 128
B_KV_RANK = 512
B_IDX_HEADS = 8
B_IDX_DIM = 128
TOPK_MAX = 256
Q_BLOCK = 128
B_Q_W = B_HEADS * B_HEAD_DIM
B_QI_W = B_IDX_HEADS * B_IDX_DIM
B_SCALE = B_HEAD_DIM ** -0.5
B_IDX_W_SCALE = (B_IDX_HEADS ** -0.5) * (B_IDX_DIM ** -0.5)
D_FF = 5632

LANES = 128
VMEM_LIMIT = 56 * 1024 * 1024
INT_MIN = -(2 ** 31)
INT_MAX = 2 ** 31 - 1
NEG_BIG = -1e30
LOG2_E = 1.4426950408889634
COUNT_WAYS = 8
ATT_HEADS_PER_STEP = 4
ATT_TILE_MULT = 4
RWKV_LANES = 256

_NT = (((1,), (1,)), ((), ()))
_TN = (((0,), (0,)), ((), ()))


def _cparams(sem):
    return pltpu.CompilerParams(dimension_semantics=sem, vmem_limit_bytes=VMEM_LIMIT)


def _dot(a, b):
    return jnp.dot(a, b, preferred_element_type=F32)


def _dot_nt(a, b):
    return lax.dot_general(a, b, _NT, preferred_element_type=F32)


def _dot_tn(a, b):
    return lax.dot_general(a, b, _TN, preferred_element_type=F32)


def _modulate(x, shift, scale):
    ms = jnp.mean(x * x, axis=-1, keepdims=True)
    return x * lax.rsqrt(ms + RMS_EPS) * (1.0 + scale) + shift


def _sigmoid(x):
    return 1.0 / (1.0 + jnp.exp(-x))


def _ada_kernel(c_ref, w_ref, b_ref, o_ref):
    c = c_ref[...]
    s = c * _sigmoid(c)
    o_ref[0] = jnp.sum(s * w_ref[0], axis=0, keepdims=True) + b_ref[0]


def _ada(c, ada_w, ada_b):
    depth, d, n = ada_w.shape
    tn = 1024
    return pl.pallas_call(
        _ada_kernel,
        grid=(depth, n // tn),
        in_specs=[
            pl.BlockSpec((d, 1), lambda l, j: (0, 0)),
            pl.BlockSpec((1, d, tn), lambda l, j: (l, 0, j)),
            pl.BlockSpec((1, 1, tn), lambda l, j: (l, 0, j)),
        ],
        out_specs=pl.BlockSpec((1, 1, tn), lambda l, j: (l, 0, j)),
        out_shape=jax.ShapeDtypeStruct((depth, 1, n), F32),
        compiler_params=_cparams(("arbitrary", "arbitrary")),
        name="ada_mod",
    )(c.reshape(d, 1), ada_w, ada_b.reshape(depth, 1, n))


def _ffn_kernel(x_ref, sh_ref, sc_ref, g_ref, w1_ref, w3_ref, w2_ref, o_ref, h_scr, acc_scr):
    f = pl.program_id(1)

    @pl.when(f == 0)
    def _():
        h_scr[...] = _modulate(x_ref[...], sh_ref[...], sc_ref[...]).astype(BF16)
        acc_scr[...] = jnp.zeros_like(acc_scr)

    h = h_scr[...]
    a = _dot(h, w1_ref[...])
    b = _dot(h, w3_ref[...])
    u = (a * _sigmoid(a) * b).astype(BF16)
    acc_scr[...] += _dot(u, w2_ref[...])

    @pl.when(f == pl.num_programs(1) - 1)
    def _():
        o_ref[...] = x_ref[...] + g_ref[...] * acc_scr[...]


def _ffn(x, shift, scale, gate, w1, w3, w2, layer):
    t, d = x.shape
    dff = w1.shape[2]
    tm, tf = 512, 512
    row = lambda i, f: (0, 0)
    return pl.pallas_call(
        _ffn_kernel,
        grid=(t // tm, dff // tf),
        in_specs=[
            pl.BlockSpec((tm, d), lambda i, f: (i, 0)),
            pl.BlockSpec((1, d), row),
            pl.BlockSpec((1, d), row),
            pl.BlockSpec((1, d), row),
            pl.BlockSpec((None, d, tf), lambda i, f: (layer, 0, f)),
            pl.BlockSpec((None, d, tf), lambda i, f: (layer, 0, f)),
            pl.BlockSpec((None, tf, d), lambda i, f: (layer, f, 0)),
        ],
        out_specs=pl.BlockSpec((tm, d), lambda i, f: (i, 0)),
        out_shape=jax.ShapeDtypeStruct((t, d), F32),
        scratch_shapes=[pltpu.VMEM((tm, d), BF16), pltpu.VMEM((tm, d), F32)],
        compiler_params=_cparams(("arbitrary", "arbitrary")),
        name="ffn",
    )(x, shift, scale, gate, w1, w3, w2)


def _proj_res_kernel(a_ref, w_ref, x_ref, g_ref, o_ref):
    o_ref[...] = x_ref[...] + g_ref[...] * _dot(a_ref[...], w_ref[...])


def _proj_res(a, w, layer, x, gate):
    t, k = a.shape
    n = w.shape[2]
    tm, tn = 512, n
    return pl.pallas_call(
        _proj_res_kernel,
        grid=(t // tm, n // tn),
        in_specs=[
            pl.BlockSpec((tm, k), lambda i, j: (i, 0)),
            pl.BlockSpec((None, k, tn), lambda i, j: (layer, 0, j), pipeline_mode=pl.Buffered(1)),
            pl.BlockSpec((tm, tn), lambda i, j: (i, j)),
            pl.BlockSpec((1, tn), lambda i, j: (0, j)),
        ],
        out_specs=pl.BlockSpec((tm, tn), lambda i, j: (i, j)),
        out_shape=jax.ShapeDtypeStruct((t, n), F32),
        compiler_params=_cparams(("arbitrary", "arbitrary")),
        name="proj_res",
    )(a, w, x, gate)


DSA_TN = 512
DSA_HEADS_PER_TILE = DSA_TN // B_HEAD_DIM
DSA_Q_TILES = B_Q_W // DSA_TN
DSA_KV_TILE = DSA_Q_TILES
DSA_QI_TILE0 = DSA_KV_TILE + B_KV_RANK // DSA_TN
DSA_QI_TILES = B_QI_W // DSA_TN
DSA_KI_TILE = DSA_QI_TILE0 + DSA_QI_TILES


def _dsa_front_kernel(x_ref, sh_ref, sc_ref, w_ref, kvn_ref, kin_ref,
                      q_ref, qi_ref, ckv_ref, ckvt_ref, kidx_ref, wrow_ref, h_scr):
    j = pl.program_id(1)
    nbk = x_ref.shape[0] // Q_BLOCK

    @pl.when(j == 0)
    def _():
        h_scr[...] = _modulate(x_ref[...], sh_ref[...], sc_ref[...]).astype(BF16)

    y = _dot(h_scr[...], w_ref[...])

    def put_heads(ref):
        for b in range(nbk):
            for h in range(DSA_HEADS_PER_TILE):
                ref[b, h] = y[b * Q_BLOCK:(b + 1) * Q_BLOCK,
                              h * B_HEAD_DIM:(h + 1) * B_HEAD_DIM].astype(BF16)

    @pl.when(j < DSA_Q_TILES)
    def _():
        put_heads(q_ref)

    @pl.when(j == DSA_KV_TILE)
    def _():
        ckv = y * lax.rsqrt(jnp.mean(y * y, axis=-1, keepdims=True) + RMS_EPS) * kvn_ref[...]
        ckv_ref[...] = ckv.astype(BF16)
        ckvt_ref[...] = ckv.T.astype(BF16)

    @pl.when((j >= DSA_QI_TILE0) & (j < DSA_KI_TILE))
    def _():
        put_heads(qi_ref)

    @pl.when(j == DSA_KI_TILE)
    def _():
        kidx = y[:, :B_IDX_DIM]
        kidx = kidx * lax.rsqrt(jnp.mean(kidx * kidx, axis=-1, keepdims=True) + RMS_EPS) * kin_ref[...]
        kidx_ref[...] = kidx.astype(BF16)
        wi = y[:, B_IDX_DIM:B_IDX_DIM + LANES] * B_IDX_W_SCALE
        for b in range(nbk):
            wrow_ref[b] = wi[b * Q_BLOCK:(b + 1) * Q_BLOCK, :].T[:B_IDX_HEADS, :]


def _dsa_front(x, shift, scale, w_in, kv_norm, kidx_norm):
    t, d = x.shape
    assert B_HEAD_DIM == B_IDX_DIM and B_KV_RANK == DSA_TN and w_in.shape[1] <= (DSA_KI_TILE + 1) * DSA_TN
    n_tiles = DSA_KI_TILE + 1
    w = jnp.pad(w_in, ((0, 0), (0, n_tiles * DSA_TN - w_in.shape[1]))).astype(BF16)
    tm = 512
    nbk = tm // Q_BLOCK
    nb = t // Q_BLOCK
    row = lambda i, j: (0, 0)
    hpt = DSA_HEADS_PER_TILE
    return pl.pallas_call(
        _dsa_front_kernel,
        grid=(t // tm, n_tiles),
        in_specs=[
            pl.BlockSpec((tm, d), lambda i, j: (i, 0)),
            pl.BlockSpec((1, d), row),
            pl.BlockSpec((1, d), row),
            pl.BlockSpec((d, DSA_TN), lambda i, j: (0, j)),
            pl.BlockSpec((1, B_KV_RANK), row),
            pl.BlockSpec((1, B_IDX_DIM), row),
        ],
        out_specs=[
            pl.BlockSpec((nbk, hpt, Q_BLOCK, B_HEAD_DIM),
                         lambda i, j: (i, jnp.minimum(j, DSA_Q_TILES - 1), 0, 0)),
            pl.BlockSpec((nbk, hpt, Q_BLOCK, B_IDX_DIM),
                         lambda i, j: (i, jnp.clip(j - DSA_QI_TILE0, 0, DSA_QI_TILES - 1), 0, 0)),
            pl.BlockSpec((tm, B_KV_RANK), lambda i, j: (i, 0)),
            pl.BlockSpec((B_KV_RANK, tm), lambda i, j: (0, i)),
            pl.BlockSpec((tm, B_IDX_DIM), lambda i, j: (i, 0)),
            pl.BlockSpec((nbk, B_IDX_HEADS, Q_BLOCK), lambda i, j: (i, 0, 0)),
        ],
        out_shape=[
            jax.ShapeDtypeStruct((nb, B_HEADS, Q_BLOCK, B_HEAD_DIM), BF16),
            jax.ShapeDtypeStruct((nb, B_IDX_HEADS, Q_BLOCK, B_IDX_DIM), BF16),
            jax.ShapeDtypeStruct((t, B_KV_RANK), BF16),
            jax.ShapeDtypeStruct((B_KV_RANK, t), BF16),
            jax.ShapeDtypeStruct((t, B_IDX_DIM), BF16),
            jax.ShapeDtypeStruct((nb, B_IDX_HEADS, Q_BLOCK), F32),
        ],
        scratch_shapes=[pltpu.VMEM((tm, d), BF16)],
        compiler_params=_cparams(("arbitrary", "arbitrary")),
        name="dsa_front",
    )(x, shift, scale, w, kv_norm, kidx_norm)


def _rwkv_front_kernel(has_vres, x_ref, sh_ref, sc_ref, mu_ref, wrkv_ref, w1_ref, a1_ref, g1_ref,
                       *rest):
    if has_vres:
        v1_ref, rkv_ref, zw_ref, za_ref, zg_ref, zv_ref, carry = rest
    else:
        rkv_ref, zw_ref, za_ref, zg_ref, carry = rest
    i = pl.program_id(0)
    tm = x_ref.shape[0]

    @pl.when(i == 0)
    def _():
        carry[...] = jnp.zeros_like(carry)

    h = _modulate(x_ref[...], sh_ref[...], sc_ref[...])
    prev_last = carry[7:8, :]
    rolled = pltpu.roll(h, 1, axis=0)
    rows = lax.broadcasted_iota(I32, h.shape, 0)
    h_prev = jnp.where(rows == 0, prev_last, rolled)
    carry[...] = h[tm - 8:tm, :]
    dx = h_prev - h

    def mix(j):
        return (h + dx * mu_ref[j:j + 1, :]).astype(BF16)

    xv = mix(3)
    rkv_ref[0] = _dot(mix(0), wrkv_ref[0])
    rkv_ref[1] = _dot(mix(2), wrkv_ref[1])
    rkv_ref[2] = _dot(xv, wrkv_ref[2])
    zw_ref[...] = jnp.tanh(_dot(mix(1), w1_ref[...])).astype(BF16)
    za_ref[...] = _dot(mix(4), a1_ref[...]).astype(BF16)
    zg_ref[...] = _sigmoid(_dot(mix(5), g1_ref[...])).astype(BF16)
    if has_vres:
        zv_ref[...] = _dot(xv, v1_ref[...]).astype(BF16)


def _pad_cols(w):
    r = w.shape[1]
    return jnp.pad(w, ((0, 0), (0, -(-r // LANES) * LANES - r))).astype(BF16)


def _pad_rows(w):
    r = w.shape[0]
    return jnp.pad(w, ((0, -(-r // LANES) * LANES - r), (0, 0))).astype(BF16)


def _rwkv_front(x, shift, scale, mu, w_rkv, layer, w1, a1, g1, v1):
    t, d = x.shape
    tm = 256
    has_vres = v1 is not None
    once = pl.Buffered(1)
    lora_ws = [w1, a1, g1] + ([v1] if has_vres else [])
    const2 = lambda i: (0, 0)
    in_specs = [
        pl.BlockSpec((tm, d), lambda i: (i, 0)),
        pl.BlockSpec((1, d), const2),
        pl.BlockSpec((1, d), const2),
        pl.BlockSpec((6, d), const2),
        pl.BlockSpec((None, 3, d, d), lambda i: (layer, 0, 0, 0), pipeline_mode=once),
    ] + [pl.BlockSpec(w.shape, const2, pipeline_mode=once) for w in lora_ws]
    out_specs = [pl.BlockSpec((3, tm, d), lambda i: (0, i, 0))]
    out_specs += [pl.BlockSpec((tm, w.shape[1]), lambda i: (i, 0)) for w in lora_ws]
    out_shape = [jax.ShapeDtypeStruct((3, t, d), F32)]
    out_shape += [jax.ShapeDtypeStruct((t, w.shape[1]), BF16) for w in lora_ws]
    return pl.pallas_call(
        functools.partial(_rwkv_front_kernel, has_vres),
        grid=(t // tm,),
        in_specs=in_specs,
        out_specs=out_specs,
        out_shape=out_shape,
        scratch_shapes=[pltpu.VMEM((8, d), F32)],
        compiler_params=_cparams(("arbitrary",)),
        name="rwkv_front",
    )(x, shift, scale, mu, w_rkv, *lora_ws)


def _split2(x):
    hi = x.astype(BF16)
    lo = (x - hi.astype(F32)).astype(BF16)
    return hi, lo


def _bdot(a, b):
    return lax.dot_general(a, b, (((2,), (1,)), ((0,), (0,))), preferred_element_type=F32)


def _bdot_nt(a, b):
    return lax.dot_general(a, b, (((2,), (2,)), ((0,), (0,))), preferred_element_type=F32)


def _bdot_tn(a, b):
    return lax.dot_general(a, b, (((1,), (1,)), ((0,), (0,))), preferred_element_type=F32)


def _rwkv_chunk_kernel(has_vres, n_chunks, *refs):
    if has_vres:
        (r_ref, k_ref, v_ref, zw_ref, za_ref, zg_ref, zv_ref, vf_ref,
         w2_ref, a2_ref, g2_ref, v2_ref,
         w0_ref, a0_ref, kk_ref, ka_ref, rk_ref, lnw_ref, lnb_ref, v0_ref,
         o_ref, s_scr) = refs
    else:
        (r_ref, k_ref, v_ref, zw_ref, za_ref, zg_ref,
         w2_ref, a2_ref, g2_ref,
         w0_ref, a0_ref, kk_ref, ka_ref, rk_ref, lnw_ref, lnb_ref,
         o_ref, s_scr) = refs

    L = CHUNK
    W = o_ref.shape[1]
    NH = W // A_HEAD
    G = n_chunks
    R = G * L

    @pl.when(pl.program_id(1) == 0)
    def _():
        s_scr[...] = jnp.zeros_like(s_scr)

    lane = lax.broadcasted_iota(I32, (1, W), 1)
    head_masks = [((lane // A_HEAD) == e).astype(F32) for e in range(NH)]
    rr = lax.broadcasted_iota(I32, (L, L), 0)
    cc = lax.broadcasted_iota(I32, (L, L), 1)
    rr2 = lax.broadcasted_iota(I32, (L, 2 * L), 0)
    cc2 = lax.broadcasted_iota(I32, (L, 2 * L), 1) % L
    low_strict2 = (rr2 > cc2)[None]
    low_incl2 = (rr2 >= cc2)[None]
    eye_l = (rr == cc).astype(F32)[None]
    diag_blk = ((rr >> 3) == (cc >> 3))[None]
    merge_blks = [(((rr >> (s + 1)) == (cc >> (s + 1))) & ((rr >> s) > (cc >> s)))[None]
                  for s in (3, 4, 5)]
    r2 = lax.broadcasted_iota(I32, (W, W), 0)
    c2 = lax.broadcasted_iota(I32, (W, W), 1)
    same_head = (r2 // A_HEAD) == (c2 // A_HEAD)
    seg_ones = same_head.astype(BF16)
    r3 = lax.broadcasted_iota(I32, (R, R), 0)
    c3 = lax.broadcasted_iota(I32, (R, R), 1)
    tri_ones = (((r3 // L) == (c3 // L)) & (r3 >= c3)).astype(BF16)

    def segsum(x):
        hi, lo = _split2(x)
        return _dot(hi, seg_ones) + _dot(lo, seg_ones)

    r = r_ref[0]
    k = k_ref[0]
    v = v_ref[0]
    z = -(w0_ref[...] + _dot(zw_ref[...], w2_ref[...]))
    softplus = jnp.maximum(z, 0.0) + jnp.log(1.0 + jnp.exp(-jnp.abs(z)))
    logw = -jnp.exp(-softplus - 0.5)
    a = _sigmoid(a0_ref[...] + _dot(za_ref[...], a2_ref[...]))
    if has_vres:
        v = v + (vf_ref[0] - v) * _sigmoid(v0_ref[...] + _dot(zv_ref[...], v2_ref[...]))
    kk = k * kk_ref[...]
    kk = kk / jnp.maximum(jnp.sqrt(segsum(kk * kk)), 1e-12)
    k = k * (1.0 + (a - 1.0) * ka_ref[...])

    h1, h2 = _split2(logw)
    cum = _dot(tri_ones, h1) + _dot(tri_ones, h2)
    w_in = jnp.exp(cum)
    w_inv = jnp.exp(-cum)
    a_t = -kk * jnp.exp(cum - logw)
    b_t = kk * a * w_inv
    k_t = k * w_inv
    r_t = r * w_in

    def c3d(x):
        return x.reshape(G, L, W)

    def per_head(x):
        x3 = c3d(x)
        return jnp.concatenate([x3 * hm for hm in head_masks], axis=0).astype(BF16)

    def head_sum(x):
        out = x[:G]
        for e in range(1, NH):
            out = out + x[e * G:(e + 1) * G]
        return out

    a_e = per_head(a_t)
    r_e = per_head(r_t)
    v_e = per_head(v)
    bk_3 = jnp.concatenate([c3d(b_t), c3d(k_t)], axis=1).astype(BF16)
    bk_e = jnp.concatenate([bk_3] * NH, axis=0)
    zeros_e = jnp.zeros_like(v_e)

    x_a = jnp.where(low_strict2, _bdot_nt(a_e, bk_e), 0.0)
    x_r = jnp.where(low_incl2, _bdot_nt(r_e, bk_e), 0.0).astype(BF16)
    x_ab = x_a[:, :, :L]

    xp = jnp.where(diag_blk, x_ab, 0.0)
    tm = eye_l + xp
    for _ in range(2):
        xb = xp.astype(BF16)
        xp = _bdot(xb, xb)
        tm = tm + _bdot(tm.astype(BF16), xp.astype(BF16))
    for off_blk in merge_blks:
        tb = tm.astype(BF16)
        x_off = jnp.where(off_blk, x_ab, 0.0).astype(BF16)
        tm = tm + _bdot(_bdot(tb, x_off).astype(BF16), tb)
    tb = tm.astype(BF16)

    p_e = _bdot(tb, a_e)
    tx = _bdot(tb, x_a.astype(BF16)).astype(BF16)
    q_e = _bdot(tx, jnp.concatenate([zeros_e, v_e], axis=1))
    p_eb = p_e.astype(BF16)
    q_eb = q_e.astype(BF16)
    rp_e = _bdot(x_r, jnp.concatenate([p_eb, zeros_e], axis=1))
    y0_e = _bdot(x_r, jnp.concatenate([q_eb, v_e], axis=1))
    p_sum = head_sum(p_e)
    q_sum = head_sum(q_e)
    rp = (c3d(r_t) + head_sum(rp_e)).astype(BF16)
    y0 = head_sum(y0_e)

    b_3 = bk_3[:, :L, :]
    w_last = c3d(w_in)[:, L - 1:L, :]
    m_lr = (jnp.where(same_head[None], _bdot_tn(p_sum.astype(BF16), b_3), 0.0) * w_last).astype(BF16)
    qv = jnp.concatenate([q_sum, c3d(v)], axis=1).astype(BF16)
    c_bd = jnp.where(same_head[None], _bdot_tn(qv, bk_3), 0.0) * w_last

    s = s_scr[...]
    ys = []
    for c in range(G):
        s_b = s.astype(BF16)
        ys.append(_dot_nt(rp[c], s_b) + y0[c])
        s = s * w_last[c] + _dot(s_b, m_lr[c]) + c_bd[c]
    s_scr[...] = s
    y = jnp.concatenate(ys, axis=0)

    mean = segsum(y) * (1.0 / A_HEAD)
    dlt = y - mean
    var = segsum(dlt * dlt) * (1.0 / A_HEAD)
    yn = dlt * lax.rsqrt(var + A_GN_EPS) * lnw_ref[...] + lnb_ref[...]
    bonus = segsum(r * k * rk_ref[...]) * v
    o_ref[...] = ((yn + bonus) * _dot(zg_ref[...], g2_ref[...])).astype(BF16)


def _rwkv_chunks(rkv, zs, ups, rkv_first, w0, a0, k_k, k_a, r_k, ln_w, ln_b, v0):
    _, t, d = rkv.shape
    n_chunks = 8
    tm = n_chunks * CHUNK
    has_vres = len(zs) == 4
    blk = lambda p, c: (c, p)
    prm = lambda p, c: (0, p)
    w = RWKV_LANES
    slab = lambda j: pl.BlockSpec((1, tm, w), lambda p, c: (j, c, p))
    in_specs = [slab(0), slab(1), slab(2)]
    in_specs += [pl.BlockSpec((tm, z.shape[1]), lambda p, c: (c, 0)) for z in zs]
    if has_vres:
        in_specs += [slab(2)]
    in_specs += [pl.BlockSpec((u.shape[0], w), prm) for u in ups]
    in_specs += [pl.BlockSpec((1, w), prm)] * (8 if has_vres else 7)
    args = [rkv, rkv, rkv] + list(zs)
    if has_vres:
        args += [rkv_first]
    args += list(ups)
    args += [w0, a0, k_k, k_a, r_k, ln_w, ln_b]
    if has_vres:
        args += [v0]
    return pl.pallas_call(
        functools.partial(_rwkv_chunk_kernel, has_vres, n_chunks),
        grid=(d // w, t // tm),
        in_specs=in_specs,
        out_specs=pl.BlockSpec((tm, w), blk),
        out_shape=jax.ShapeDtypeStruct((t, d), BF16),
        scratch_shapes=[pltpu.VMEM((w, w), F32)],
        compiler_params=_cparams(("arbitrary", "arbitrary")),
        name="rwkv_chunks",
    )(*args)


def _dsa_attn_kernel(topk, tk, idx_bits,
                     q_ref, qi_ref, wrow_ref, kidx_ref, ckv_ref, ckvt_ref, wuk_ref, wuv_ref, o_ref,
                     key_scr, qlat_scr, acc_scr, m_scr, l_scr, j0_scr):
    i = pl.program_id(0)
    QB = Q_BLOCK
    cw = ATT_HEADS_PER_STEP * QB
    n_kt = (i * QB + QB + tk - 1) // tk
    qpos = i * QB + lax.broadcasted_iota(I32, (1, QB), 1)
    limit = (qpos // CHUNK + 1) * CHUNK

    def key_pos(off):
        return off + lax.broadcasted_iota(I32, (tk, QB), 0)

    for h in range(B_HEADS):
        ql = _dot(q_ref[0, h], wuk_ref[h]) * (B_SCALE * LOG2_E)
        qlat_scr[h * QB:(h + 1) * QB, :] = ql.astype(BF16)

    qi = qi_ref[0].reshape(B_IDX_HEADS * QB, B_IDX_DIM)
    wrow = wrow_ref[0]

    def score_tile(off, size):
        rel = jnp.maximum(_dot_nt(kidx_ref[pl.ds(off, size), :], qi), 0.0)
        sc = rel[:, 0:QB] * wrow[0:1, :]
        for h in range(1, B_IDX_HEADS):
            sc = sc + rel[:, h * QB:(h + 1) * QB] * wrow[h:h + 1, :]
        sc = jnp.where(sc == 0.0, 0.0, sc)
        bits = pltpu.bitcast(sc, I32)
        key = bits ^ ((bits >> 31) & INT_MAX)
        kpos = off + lax.broadcasted_iota(I32, (size, QB), 0)
        key_scr[pl.ds(off, size), :] = jnp.where(kpos < limit, key, INT_MIN)

    def score_pair(j, carry):
        score_tile(pl.multiple_of(j * (2 * tk), 2 * tk), 2 * tk)
        return carry

    lax.fori_loop(0, n_kt // 2, score_pair, 0)

    @pl.when(n_kt % 2 == 1)
    def _():
        score_tile(pl.multiple_of((n_kt - 1) * tk, tk), tk)

    def count_keys(pred):
        def body(kt, acc):
            off = pl.multiple_of(kt * tk, tk)
            hit = jnp.where(pred(key_scr[pl.ds(off, tk), :], off), 1.0, 0.0)
            return acc + jnp.sum(hit.reshape(tk // (8 * COUNT_WAYS), COUNT_WAYS, 8, QB), axis=0)
        acc = lax.fori_loop(0, n_kt, body, jnp.zeros((COUNT_WAYS, 8, QB), F32))
        return jnp.sum(jnp.sum(acc, axis=0), axis=0, keepdims=True)

    def bit_body(b, st):
        thr_u, cnt_ge = st
        cand = thr_u | jnp.left_shift(jnp.int32(1), 31 - b)
        cand_s = cand ^ INT_MIN
        cnt = count_keys(lambda key, off: key >= cand_s)
        take = cnt >= topk
        return jnp.where(take, cand, thr_u), jnp.where(take, cnt, cnt_ge)

    n_scanned = jnp.zeros((1, QB), F32) + (n_kt * tk).astype(F32)
    thr_u, cnt_ge = lax.fori_loop(0, 32, bit_body, (jnp.zeros((1, QB), I32), n_scanned))
    thr = thr_u ^ INT_MIN

    j0_scr[...] = jnp.full((1, QB), INT_MAX, I32)
    excess = jnp.max(jnp.where(thr > INT_MIN, cnt_ge, 0.0)) > topk

    @pl.when(excess)
    def _():
        need = topk - count_keys(lambda key, off: key > thr)

        def tie_body(b, j0):
            cand = j0 | jnp.left_shift(jnp.int32(1), idx_bits - 1 - b)
            cnt = count_keys(lambda key, off: (key == thr) & (key_pos(off) < cand))
            return jnp.where(cnt < need, cand, j0)
        j0_scr[...] = lax.fori_loop(0, idx_bits, tie_body, jnp.zeros((1, QB), I32))

    j0 = j0_scr[...]

    m_scr[...] = jnp.full(m_scr.shape, NEG_BIG, F32)
    l_scr[...] = jnp.zeros_like(l_scr)
    acc_scr[...] = jnp.zeros_like(acc_scr)

    def att_tile(off, size):
        key = key_scr[pl.ds(off, size), :]
        kpos = off + lax.broadcasted_iota(I32, (size, QB), 0)
        sel = ((key > thr) | ((key == thr) & (kpos <= j0))) & (key > INT_MIN)
        bias = jnp.where(sel, 0.0, NEG_BIG)
        bias = jnp.concatenate([bias] * ATT_HEADS_PER_STEP, axis=1)
        ckv_t = ckv_ref[pl.ds(off, size), :]
        ckvt_t = ckvt_ref[:, pl.ds(off, size)]
        n_groups = B_HEADS // ATT_HEADS_PER_STEP

        def logits(c):
            return _dot_nt(ckv_t, qlat_scr[c * cw:(c + 1) * cw, :]) + bias

        s_next = logits(0)
        for c in range(n_groups):
            cols = slice(c * cw, (c + 1) * cw)
            s = s_next
            if c + 1 < n_groups:
                s_next = logits(c + 1)
            m_old = m_scr[:, cols]
            m_new = jnp.maximum(m_old, jnp.max(s, axis=0, keepdims=True))
            alpha = jnp.exp2(m_old - m_new)
            p = jnp.exp2(s - m_new)
            l_scr[:, cols] = alpha * l_scr[:, cols] + jnp.sum(p, axis=0, keepdims=True)
            acc_scr[:, cols] = alpha * acc_scr[:, cols] + _dot(ckvt_t, p.astype(BF16))
            m_scr[:, cols] = m_new

    big = ATT_TILE_MULT * tk

    def att_big(j, carry):
        att_tile(pl.multiple_of(j * big, big), big)
        return carry

    lax.fori_loop(0, n_kt // ATT_TILE_MULT, att_big, 0)
    done = (n_kt // ATT_TILE_MULT) * ATT_TILE_MULT
    mult = ATT_TILE_MULT // 2
    while mult >= 1:
        @pl.when((n_kt & mult) != 0)
        def _(done=done, mult=mult):
            att_tile(pl.multiple_of(done * tk, tk), mult * tk)
        done = done + (n_kt & mult)
        mult //= 2

    o_lat_t = (acc_scr[...] / l_scr[...]).astype(BF16)
    for h in range(B_HEADS):
        oh = _dot_tn(o_lat_t[:, h * QB:(h + 1) * QB], wuv_ref[h])
        o_ref[:, h * B_HEAD_DIM:(h + 1) * B_HEAD_DIM] = oh.astype(BF16)


def _dsa_attn(q, qi, wrow, kidx, ckv, ckvt, w_uk, w_uv):
    nb = q.shape[0]
    t = kidx.shape[0]
    tk = 512
    topk = min(TOPK_MAX, t // 4)
    idx_bits = max(1, (t - 1).bit_length())
    const2 = lambda i: (0, 0)
    const3 = lambda i: (0, 0, 0)
    hq = B_HEADS * Q_BLOCK
    once = pl.Buffered(1)
    return pl.pallas_call(
        functools.partial(_dsa_attn_kernel, topk, tk, idx_bits),
        grid=(nb,),
        in_specs=[
            pl.BlockSpec((1, B_HEADS, Q_BLOCK, B_HEAD_DIM), lambda i: (i, 0, 0, 0)),
            pl.BlockSpec((1, B_IDX_HEADS, Q_BLOCK, B_IDX_DIM), lambda i: (i, 0, 0, 0)),
            pl.BlockSpec((1, B_IDX_HEADS, Q_BLOCK), lambda i: (i, 0, 0)),
            pl.BlockSpec((t, B_IDX_DIM), const2, pipeline_mode=once),
            pl.BlockSpec((t, B_KV_RANK), const2, pipeline_mode=once),
            pl.BlockSpec((B_KV_RANK, t), const2, pipeline_mode=once),
            pl.BlockSpec((B_HEADS, B_HEAD_DIM, B_KV_RANK), const3, pipeline_mode=once),
            pl.BlockSpec((B_HEADS, B_KV_RANK, B_HEAD_DIM), const3, pipeline_mode=once),
        ],
        out_specs=pl.BlockSpec((Q_BLOCK, B_Q_W), lambda i: (i, 0)),
        out_shape=jax.ShapeDtypeStruct((t, B_Q_W), BF16),
        scratch_shapes=[
            pltpu.VMEM((t, Q_BLOCK), I32),
            pltpu.VMEM((hq, B_KV_RANK), BF16),
            pltpu.VMEM((B_KV_RANK, hq), F32),
            pltpu.VMEM((1, hq), F32),
            pltpu.VMEM((1, hq), F32),
            pltpu.VMEM((1, Q_BLOCK), I32),
        ],
        compiler_params=_cparams(("arbitrary",)),
        name="dsa_attn",
    )(q, qi, wrow, kidx, ckv, ckvt, w_uk, w_uv)


def _final_norm_kernel(x_ref, w_ref, o_ref):
    x = x_ref[...]
    o_ref[...] = x * lax.rsqrt(jnp.mean(x * x, axis=-1, keepdims=True) + RMS_EPS) * w_ref[...]


def _final_norm(x, w):
    t, d = x.shape
    tm = 512
    return pl.pallas_call(
        _final_norm_kernel,
        grid=(t // tm,),
        in_specs=[pl.BlockSpec((tm, d), lambda i: (i, 0)), pl.BlockSpec((1, d), lambda i: (0, 0))],
        out_specs=pl.BlockSpec((tm, d), lambda i: (i, 0)),
        out_shape=jax.ShapeDtypeStruct((t, d), F32),
        compiler_params=_cparams(("arbitrary",)),
        name="final_norm",
    )(x, w)


def _row(v):
    return v.reshape(1, -1)


def _rwkv_layer(x, shift, scale, gate, rkv_first, layer, mu, w_rkv, w0, w1, w2, a0, a1, a2,
                g1, g2, k_k, k_a, r_k, ln_w, ln_b, w_o, vres):
    ups = [_pad_rows(w2), _pad_rows(a2), _pad_rows(g2)]
    if vres is None:
        v0, v1 = None, None
    else:
        v0, v1, v2 = vres
        v0, v1 = _row(v0), _pad_cols(v1)
        ups.append(_pad_rows(v2))
    outs = _rwkv_front(x, shift, scale, mu, w_rkv, layer, _pad_cols(w1), _pad_cols(a1),
                       _pad_cols(g1), v1)
    rkv, zs = outs[0], outs[1:]
    if vres is None:
        rkv_first = rkv
    y = _rwkv_chunks(rkv, zs, ups, rkv_first, _row(w0), _row(a0), _row(k_k), _row(k_a),
                     _row(r_k), _row(ln_w), _row(ln_b), v0)
    return _proj_res(y, w_o, layer, x, gate), rkv_first


def _dsa_layer(x, shift, scale, gate, layer, w_in, kv_norm, kidx_norm, w_uk, w_uv, w_o):
    q, qi, ckv, ckvt, kidx, wrow = _dsa_front(x, shift, scale, w_in, _row(kv_norm), _row(kidx_norm))
    o = _dsa_attn(q, qi, wrow, kidx, ckv, ckvt, w_uk.astype(BF16), w_uv.astype(BF16))
    return _proj_res(o, w_o, layer, x, gate)


def kernel(x, c, ada_w, ada_b, a_mu, a_w_rkv, a_w0, a_w1, a_w2, a_a0, a_a1, a_a2, a_v0, a_v1, a_v2, a_g1, a_g2, a_k_k, a_k_a, a_r_k, a_ln_w, a_ln_b, a_w_o, b_w_in, b_kv_norm, b_kidx_norm, b_w_uk, b_w_uv, b_w_o, f_w1, f_w3, f_w2, final_norm):
    b, t, d = x.shape
    assert b == 1 and d == D_MODEL
    mod_all = _ada(c, ada_w, ada_b)
    xs = x.reshape(t, d)
    v_first = None
    a_w_rkv, a_w_o, b_w_o = a_w_rkv.astype(BF16), a_w_o.astype(BF16), b_w_o.astype(BF16)
    f_w1, f_w3, f_w2 = f_w1.astype(BF16), f_w3.astype(BF16), f_w2.astype(BF16)
    for i in range(DEPTH):
        shift1, scale1, gate1, shift2, scale2, gate2 = (
            mod_all[i, :, s * d:(s + 1) * d] for s in range(6))
        j = i // 2
        if i % 2 == 0:
            vres = None if j == 0 else (a_v0[j - 1], a_v1[j - 1], a_v2[j - 1])
            xs, v_first = _rwkv_layer(
                xs, shift1, scale1, gate1, v_first, j, a_mu[j], a_w_rkv, a_w0[j], a_w1[j], a_w2[j],
                a_a0[j], a_a1[j], a_a2[j], a_g1[j], a_g2[j], a_k_k[j], a_k_a[j], a_r_k[j],
                a_ln_w[j], a_ln_b[j], a_w_o, vres)
        else:
            xs = _dsa_layer(xs, shift1, scale1, gate1, j, b_w_in[j], b_kv_norm[j], b_kidx_norm[j],
                            b_w_uk[j], b_w_uv[j], b_w_o)
        xs = _ffn(xs, shift2, scale2, gate2, f_w1, f_w3, f_w2, i)
    return _final_norm(xs, _row(final_norm)).reshape(b, t, d)
```
